```python
import jax
import jax.numpy as jnp
from jax import lax
import numpy as np

D_MODEL = 1024
BATCH = 1
SEQ = 16384
DEPTH = 2
DEC_BATCH = 32
DEC_SEQ = 4
PAST_LEN = 16384
PAGE_SIZE = 128

N_MIXERS = 2
N_CONV_LAYERS = (DEPTH + N_MIXERS - 1) // N_MIXERS
N_SB_LAYERS = DEPTH // N_MIXERS
MIX_WIDTH = D_MODEL
MEM_WIDTH = MIX_WIDTH // 4
MEM_HEADS = 4
MEM_HEAD_DIM = MEM_WIDTH // MEM_HEADS
MEM_TOKENS = 256
CONV_CH = MIX_WIDTH - MEM_WIDTH
CONV_WIDTH = 31
SB_HEAD_DIM = 64
SB_WIDTH = MIX_WIDTH - MEM_WIDTH
SB_HEADS = SB_WIDTH // SB_HEAD_DIM
SB_LOGIT_BIAS_INIT = -8.0
Q_BLOCK = 128
N_EXPERTS = 16
N_GROUPS = 4
EXPERTS_PER_GROUP = N_EXPERTS // N_GROUPS
TOP_K = 2
EXPERT_FF = D_MODEL // 2
ALPHA = (2.0 * DEPTH) ** 0.25
BETA = (8.0 * DEPTH) ** -0.25
LN_EPS = 1e-5
SB_SCALE = SB_HEAD_DIM ** -0.5
MEM_SCALE = MEM_HEAD_DIM ** -0.5

kernel_name = 'hybrid_conformer_stickbreak_moe_decoder'


def layer_norm(x, g, b):
    x32 = x.astype(jnp.float32)
    mu = x32.mean(-1, keepdims=True)
    var = jnp.square(x32 - mu).mean(-1, keepdims=True)
    y = (x32 - mu) * lax.rsqrt(var + LN_EPS) * g.astype(jnp.float32) + b.astype(jnp.float32)
    return y.astype(x.dtype)


def split_heads(t, n_heads):
    return t.reshape(*t.shape[:-1], n_heads, t.shape[-1] // n_heads)


def memory_kv(mem, w_kv):
    k, v = jnp.split(mem @ w_kv, 2, axis=-1)
    return split_heads(k, MEM_HEADS), split_heads(v, MEM_HEADS)


def memory_attend(q, mem_k, mem_v):
    s = jnp.einsum('bthd,bmhd->bhtm', q, mem_k, preferred_element_type=jnp.float32) * MEM_SCALE
    p = jax.nn.softmax(s, axis=-1)
    o = jnp.einsum('bhtm,bmhd->bthd', p, mem_v.astype(jnp.float32))
    return o.reshape(*o.shape[:2], MEM_WIDTH).astype(q.dtype)


def conv_module(h, w_in, b_in, conv_w, conv_b, norm_g, norm_b, past):
    proj = h @ w_in + b_in
    a, gate, q_mem = jnp.split(proj, [CONV_CH, 2 * CONV_CH], axis=-1)
    u = a * jax.nn.sigmoid(gate)
    full = jnp.concatenate([past, u], axis=1)
    y = lax.conv_general_dilated(full, conv_w[:, None, :], (1,), 'VALID',
                                 dimension_numbers=('NWC', 'WIO', 'NWC'),
                                 feature_group_count=CONV_CH) + conv_b
    y = jax.nn.silu(layer_norm(y, norm_g, norm_b))
    return y, split_heads(q_mem, MEM_HEADS), full[:, -(CONV_WIDTH - 1):]


def sb_project(h, w_in, b_in):
    proj = h @ w_in + b_in
    q, k, v, q_mem = jnp.split(proj, [SB_WIDTH, 2 * SB_WIDTH, 3 * SB_WIDTH], axis=-1)
    return (split_heads(q, SB_HEADS), split_heads(k, SB_HEADS), split_heads(v, SB_HEADS),
            split_heads(q_mem, MEM_HEADS))


def sb_block(q, k, v, logit_bias, mask, carry):
    o, s_after = carry
    z = (jnp.einsum('bthd,bshd->bhts', q, k, preferred_element_type=jnp.float32) * SB_SCALE
         + logit_bias.astype(jnp.float32)[None, :, None, None])
    log_keep = jnp.where(mask, jax.nn.log_sigmoid(-z), 0.0)
    suffix = lax.cumsum(log_keep, axis=3, reverse=True) - log_keep + s_after[..., None]
    w = jnp.where(mask, jnp.exp(jax.nn.log_sigmoid(z) + suffix), 0.0)
    o = o + jnp.einsum('bhts,bshd->bthd', w, v.astype(jnp.float32))
    return o, s_after + log_keep.sum(-1)


def sb_prompt(q, k, v, logit_bias):
    b, seq_len, n_h, d_h = q.shape
    nb = seq_len // Q_BLOCK
    qb = q.reshape(b, nb, Q_BLOCK, n_h, d_h)
    kb = k.reshape(b, nb, Q_BLOCK, n_h, d_h)
    vb = v.reshape(b, nb, Q_BLOCK, n_h, d_h)
    idx = jnp.arange(Q_BLOCK)
    diag_mask = idx[None, :] < idx[:, None]
    full_mask = jnp.ones((Q_BLOCK, Q_BLOCK), bool)

    def one_query_block(i):
        qi = lax.dynamic_index_in_dim(qb, i, 1, keepdims=False)
        ki = lax.dynamic_index_in_dim(kb, i, 1, keepdims=False)
        vi = lax.dynamic_index_in_dim(vb, i, 1, keepdims=False)
        init = (jnp.zeros((b, Q_BLOCK, n_h, d_h), jnp.float32),
                jnp.zeros((b, n_h, Q_BLOCK), jnp.float32))
        carry = sb_block(qi, ki, vi, logit_bias, diag_mask, init)

        def body(j, c):
            kj = lax.dynamic_index_in_dim(kb, i - j, 1, keepdims=False)
            vj = lax.dynamic_index_in_dim(vb, i - j, 1, keepdims=False)
            return sb_block(qi, kj, vj, logit_bias, full_mask, c)

        o, _ = lax.fori_loop(1, i + 1, body, carry)
        return o.astype(q.dtype)

    out = lax.map(one_query_block, jnp.arange(nb))
    return jnp.moveaxis(out, 0, 1).reshape(b, seq_len, n_h * d_h)


def sb_sample(q, k_new, v_new, logit_bias, cache_k, cache_v, layer, page_table):
    b, t_new, n_h, d_h = q.shape
    idx = jnp.arange(t_new)
    init = (jnp.zeros((b, t_new, n_h, d_h), jnp.float32),
            jnp.zeros((b, n_h, t_new), jnp.float32))
    carry = sb_block(q, k_new, v_new, logit_bias, idx[None, :] < idx[:, None], init)
    page_mask = jnp.ones((t_new, PAGE_SIZE), bool)
    pool_k = cache_k[layer]
    pool_v = cache_v[layer]

    def step(c, phys):
        return sb_block(q, pool_k[phys], pool_v[phys], logit_bias, page_mask, c), None

    (o, _), _ = lax.scan(step, carry, page_table.T, reverse=True)
    return o.reshape(b, t_new, n_h * d_h).astype(q.dtype)


def mixer_out(mix, q_mem, mem_k, mem_v, w_out, b_out):
    merged = jnp.concatenate([mix, memory_attend(q_mem, mem_k, mem_v)], axis=-1)
    return merged @ w_out + b_out


def moe(x, w_router, b_router, w_gate, w_up, w_down):
    shp = x.shape
    xf = x.reshape(-1, shp[-1])
    logits = jnp.matmul(xf, w_router, preferred_element_type=jnp.float32) + b_router.astype(jnp.float32)
    probs = jax.nn.softmax(logits, axis=-1)
    grouped = probs.reshape(-1, N_GROUPS, EXPERTS_PER_GROUP)
    group_score = lax.top_k(grouped, TOP_K)[0].sum(-1)
    g_sel = jnp.argmax(group_score, axis=-1)
    in_group = jnp.take_along_axis(grouped, g_sel[:, None, None], axis=1)[:, 0]
    top_p, top_i = lax.top_k(in_group, TOP_K)
    gates = top_p / top_p.sum(-1, keepdims=True)
    expert_id = g_sel[:, None] * EXPERTS_PER_GROUP + top_i
    gate_full = jnp.einsum('nk,nke->ne', gates, jax.nn.one_hot(expert_id, N_EXPERTS, dtype=jnp.float32))
    out = jnp.zeros(xf.shape, jnp.float32)
    for e in range(N_EXPERTS):
        h = jax.nn.silu(xf @ w_gate[e]) * (xf @ w_up[e])
        out = out + (h @ w_down[e]).astype(jnp.float32) * gate_full[:, e:e + 1]
    return out.reshape(shp).astype(x.dtype)


def setup_inputs(seed: int = 0) -> dict:
    key = jax.random.key(seed)
    ks = iter(jax.random.split(key, 40))

    def nrm(shape, scale):
        return jax.random.normal(next(ks), shape, jnp.float32) * scale

    n_pages = PAST_LEN // PAGE_SIZE
    n_used = DEC_BATCH * n_pages
    n_pool = n_used + (n_used + 3) // 4
    page_table = jax.random.permutation(next(ks), n_pool)[:n_used].reshape(DEC_BATCH, n_pages).astype(jnp.int32)
    d_in_a = 2 * CONV_CH + MEM_WIDTH
    d_in_b = 3 * SB_WIDTH + MEM_WIDTH
    return {
        'x_prompt': nrm((BATCH, SEQ, D_MODEL), 1.0),
        'x_sample': nrm((DEC_BATCH, DEC_SEQ, D_MODEL), 1.0),
        'state_conv': nrm((N_CONV_LAYERS, DEC_BATCH, CONV_WIDTH - 1, CONV_CH), 0.5),
        'cache_sb_k': nrm((N_SB_LAYERS, n_pool, PAGE_SIZE, SB_HEADS, SB_HEAD_DIM), 1.0),
        'cache_sb_v': nrm((N_SB_LAYERS, n_pool, PAGE_SIZE, SB_HEADS, SB_HEAD_DIM), 1.0),
        'cache_mem_k': nrm((DEPTH, DEC_BATCH, MEM_TOKENS, MEM_HEADS, MEM_HEAD_DIM), 1.0),
        'cache_mem_v': nrm((DEPTH, DEC_BATCH, MEM_TOKENS, MEM_HEADS, MEM_HEAD_DIM), 1.0),
        'page_table': page_table,
        'mem_prompt': nrm((BATCH, MEM_TOKENS, D_MODEL), 1.0),
        'a_w_in': nrm((N_CONV_LAYERS, D_MODEL, d_in_a), D_MODEL ** -0.5),
        'a_b_in': nrm((N_CONV_LAYERS, d_in_a), 0.01),
        'a_conv_w': nrm((N_CONV_LAYERS, CONV_WIDTH, CONV_CH), CONV_WIDTH ** -0.5),
        'a_conv_b': nrm((N_CONV_LAYERS, CONV_CH), 0.01),
        'a_norm_g': 1.0 + nrm((N_CONV_LAYERS, CONV_CH), 0.01),
        'a_norm_b': nrm((N_CONV_LAYERS, CONV_CH), 0.01),
        'a_w_out': nrm((N_CONV_LAYERS, CONV_CH + MEM_WIDTH, D_MODEL), BETA * (CONV_CH + MEM_WIDTH) ** -0.5),
        'a_b_out': nrm((N_CONV_LAYERS, D_MODEL), 0.01),
        'b_w_in': nrm((N_SB_LAYERS, D_MODEL, d_in_b), D_MODEL ** -0.5),
        'b_b_in': nrm((N_SB_LAYERS, d_in_b), 0.01),
        'b_sb_bias': SB_LOGIT_BIAS_INIT + nrm((N_SB_LAYERS, SB_HEADS), 0.1),
        'b_w_out': nrm((N_SB_LAYERS, SB_WIDTH + MEM_WIDTH, D_MODEL), BETA * (SB_WIDTH + MEM_WIDTH) ** -0.5),
        'b_b_out': nrm((N_SB_LAYERS, D_MODEL), 0.01),
        'w_mem_kv': nrm((DEPTH, D_MODEL, 2 * MEM_WIDTH), D_MODEL ** -0.5),
        'ln_mix_g': 1.0 + nrm((DEPTH, D_MODEL), 0.01),
        'ln_mix_b': nrm((DEPTH, D_MODEL), 0.01),
        'ln_ffn_g': 1.0 + nrm((DEPTH, D_MODEL), 0.01),
        'ln_ffn_b': nrm((DEPTH, D_MODEL), 0.01),
        'w_router': nrm((D_MODEL, N_EXPERTS), D_MODEL ** -0.5),
        'b_router': nrm((N_EXPERTS,), 0.01),
        'w_exp_gate': nrm((DEPTH, N_EXPERTS, D_MODEL, EXPERT_FF), D_MODEL ** -0.5),
        'w_exp_up': nrm((DEPTH, N_EXPERTS, D_MODEL, EXPERT_FF), D_MODEL ** -0.5),
        'w_exp_down': nrm((DEPTH, N_EXPERTS, EXPERT_FF, D_MODEL), BETA * EXPERT_FF ** -0.5),
    }


def reference(x_prompt, x_sample, state_conv, cache_sb_k, cache_sb_v, cache_mem_k, cache_mem_v,
              page_table, mem_prompt, a_w_in, a_b_in, a_conv_w, a_conv_b, a_norm_g, a_norm_b,
              a_w_out, a_b_out, b_w_in, b_b_in, b_sb_bias, b_w_out, b_b_out, w_mem_kv, ln_mix_g,
              ln_mix_b, ln_ffn_g, ln_ffn_b, w_router, b_router, w_exp_gate, w_exp_up, w_exp_down):
    y_p, y_s = x_prompt, x_sample
    conv_p, conv_s, sbk_p, sbv_p, sbk_s, sbv_s, memk_p, memv_p = [], [], [], [], [], [], [], []
    for layer in range(DEPTH):
        mk_p, mv_p = memory_kv(mem_prompt, w_mem_kv[layer])
        memk_p.append(mk_p)
        memv_p.append(mv_p)
        mk_s, mv_s = cache_mem_k[layer], cache_mem_v[layer]
        i = layer // N_MIXERS
        if layer % N_MIXERS == 0:
            past_p = jnp.zeros((y_p.shape[0], CONV_WIDTH - 1, CONV_CH), y_p.dtype)
            mix_p, qm_p, st_p = conv_module(y_p, a_w_in[i], a_b_in[i], a_conv_w[i], a_conv_b[i],
                                            a_norm_g[i], a_norm_b[i], past_p)
            mix_s, qm_s, st_s = conv_module(y_s, a_w_in[i], a_b_in[i], a_conv_w[i], a_conv_b[i],
                                            a_norm_g[i], a_norm_b[i], state_conv[i])
            conv_p.append(st_p)
            conv_s.append(st_s)
            w_out, b_out = a_w_out[i], a_b_out[i]
        else:
            q, k, v, qm_p = sb_project(y_p, b_w_in[i], b_b_in[i])
            mix_p = sb_prompt(q, k, v, b_sb_bias[i])
            sbk_p.append(k)
            sbv_p.append(v)
            q, k, v, qm_s = sb_project(y_s, b_w_in[i], b_b_in[i])
            mix_s = sb_sample(q, k, v, b_sb_bias[i], cache_sb_k, cache_sb_v, i, page_table)
            sbk_s.append(k)
            sbv_s.append(v)
            w_out, b_out = b_w_out[i], b_b_out[i]
        y_p = layer_norm(ALPHA * y_p + mixer_out(mix_p, qm_p, mk_p, mv_p, w_out, b_out),
                         ln_mix_g[layer], ln_mix_b[layer])
        y_s = layer_norm(ALPHA * y_s + mixer_out(mix_s, qm_s, mk_s, mv_s, w_out, b_out),
                         ln_mix_g[layer], ln_mix_b[layer])
        y_p = layer_norm(ALPHA * y_p + moe(y_p, w_router, b_router, w_exp_gate[layer],
                                           w_exp_up[layer], w_exp_down[layer]),
                         ln_ffn_g[layer], ln_ffn_b[layer])
        y_s = layer_norm(ALPHA * y_s + moe(y_s, w_router, b_router, w_exp_gate[layer],
                                           w_exp_up[layer], w_exp_down[layer]),
                         ln_ffn_g[layer], ln_ffn_b[layer])
    return (y_p, y_s, jnp.stack(conv_p), jnp.stack(conv_s), jnp.stack(sbk_p), jnp.stack(sbv_p),
            jnp.stack(sbk_s), jnp.stack(sbv_s), jnp.stack(memk_p), jnp.stack(memv_p))
```

```python
import functools
import math

import jax
import jax.numpy as jnp
from jax import lax
from jax.experimental import pallas as pl
from jax.experimental.pallas import tpu as pltpu

LN_EPS = 1e-5
N_GROUPS = 4
TOP_K = 2
CONV_HALO = 32
LANES = 128
SB_ROWS_PER_HEAD = 8
SB_ROW_CHUNK = 32
VMEM_LIMIT = 48 * 1024 * 1024
LOG2E = math.log2(math.e)

BF16 = jnp.bfloat16
F32 = jnp.float32


def _tile(n, pref):
    if n <= pref:
        return n
    t = pref - pref % 8
    while t >= 8:
        if n % t == 0:
            return t
        t -= 8
    return n


def _params(*sem):
    return pltpu.CompilerParams(dimension_semantics=sem, vmem_limit_bytes=VMEM_LIMIT)


def _layer_norm(x, g, b):
    mu = jnp.mean(x, axis=-1, keepdims=True)
    xc = x - mu
    var = jnp.mean(xc * xc, axis=-1, keepdims=True)
    return xc * lax.rsqrt(var + LN_EPS) * g + b


def _dot(a, b):
    return jnp.dot(a, b, preferred_element_type=F32)


def _dot_nt(a, b):
    return lax.dot_general(a, b, (((1,), (1,)), ((), ())), preferred_element_type=F32)


def _linear_kernel(x_ref, w_ref, b_ref, o_ref):
    o_ref[...] = _dot(x_ref[...].astype(BF16), w_ref[...]) + b_ref[...]


def _linear(x, w_bf, b, *, name="linear"):
    m, k = x.shape
    n = w_bf.shape[1]
    tm = _tile(m, 256)
    return pl.pallas_call(
        _linear_kernel,
        grid=(m // tm,),
        in_specs=[pl.BlockSpec((tm, k), lambda i: (i, 0)),
                  pl.BlockSpec((k, n), lambda i: (0, 0)),
                  pl.BlockSpec((1, n), lambda i: (0, 0))],
        out_specs=pl.BlockSpec((tm, n), lambda i: (i, 0)),
        out_shape=jax.ShapeDtypeStruct((m, n), F32),
        compiler_params=_params("parallel"),
        name=name,
    )(x, w_bf, b.reshape(1, n))


def _linear_t_kernel(x_ref, wt_ref, o_ref):
    o_ref[...] = _dot_nt(wt_ref[...], x_ref[...].astype(BF16))


def _linear_t(x, wt_bf, *, name="linear_t"):
    m, k = x.shape
    n = wt_bf.shape[0]
    tm = _tile(m, 256)
    return pl.pallas_call(
        _linear_t_kernel,
        grid=(m // tm,),
        in_specs=[pl.BlockSpec((tm, k), lambda i: (i, 0)),
                  pl.BlockSpec((n, k), lambda i: (0, 0))],
        out_specs=pl.BlockSpec((n, tm), lambda i: (0, i)),
        out_shape=jax.ShapeDtypeStruct((n, m), F32),
        compiler_params=_params("parallel"),
        name=name,
    )(x, wt_bf)


def _sb_in_proj_kernel(x_ref, wq_ref, wkt_ref, wvt_ref, wm_ref, bq_ref, bk_ref, bv_ref, bm_ref,
                       q_ref, kt_ref, vt_ref, ktb_ref, vtb_ref, qm_ref, *, q_scale):
    xb = x_ref[...].astype(BF16)
    q_ref[...] = ((_dot(xb, wq_ref[...]) + bq_ref[...]) * q_scale).astype(BF16)
    kt = _dot_nt(wkt_ref[...], xb) + bk_ref[...]
    kt_ref[...] = kt
    ktb_ref[...] = kt.astype(BF16)
    vt = _dot_nt(wvt_ref[...], xb) + bv_ref[...]
    vt_ref[...] = vt
    vtb_ref[...] = vt.astype(BF16)
    qm_ref[...] = _dot(xb, wm_ref[...]) + bm_ref[...]


def _sb_in_proj(x, w_in, b_in, sb_w, q_scale):
    n, d = x.shape
    wm = w_in.shape[1] - 3 * sb_w
    tm = _tile(n, 256)
    wq = w_in[:, :sb_w].astype(BF16)
    wkt = w_in[:, sb_w:2 * sb_w].T.astype(BF16)
    wvt = w_in[:, 2 * sb_w:3 * sb_w].T.astype(BF16)
    wmem = w_in[:, 3 * sb_w:].astype(BF16)
    const = lambda shape: pl.BlockSpec(shape, lambda i: (0, 0))
    return pl.pallas_call(
        functools.partial(_sb_in_proj_kernel, q_scale=q_scale),
        grid=(n // tm,),
        in_specs=[pl.BlockSpec((tm, d), lambda i: (i, 0)),
                  const((d, sb_w)), const((sb_w, d)), const((sb_w, d)), const((d, wm)),
                  const((1, sb_w)), const((sb_w, 1)), const((sb_w, 1)), const((1, wm))],
        out_specs=[pl.BlockSpec((tm, sb_w), lambda i: (i, 0)),
                   pl.BlockSpec((sb_w, tm), lambda i: (0, i)),
                   pl.BlockSpec((sb_w, tm), lambda i: (0, i)),
                   pl.BlockSpec((sb_w, tm), lambda i: (0, i)),
                   pl.BlockSpec((sb_w, tm), lambda i: (0, i)),
                   pl.BlockSpec((tm, wm), lambda i: (i, 0))],
        out_shape=[jax.ShapeDtypeStruct((n, sb_w), BF16),
                   jax.ShapeDtypeStruct((sb_w, n), F32),
                   jax.ShapeDtypeStruct((sb_w, n), F32),
                   jax.ShapeDtypeStruct((sb_w, n), BF16),
                   jax.ShapeDtypeStruct((sb_w, n), BF16),
                   jax.ShapeDtypeStruct((n, wm), F32)],
        compiler_params=_params("parallel"),
        name="sb_in_proj",
    )(x, wq, wkt, wvt, wmem,
      b_in[:sb_w].reshape(1, sb_w), b_in[sb_w:2 * sb_w].reshape(sb_w, 1),
      b_in[2 * sb_w:3 * sb_w].reshape(sb_w, 1), b_in[3 * sb_w:].reshape(1, wm))


def _conv_prompt_kernel(a_ref, gate_ref, w_ref, cb_ref, ng_ref, nb_ref,
                        mix_ref, st_ref, f_ref, *, tile, width, chunk):
    i = pl.program_id(0)
    halo = CONV_HALO

    @pl.when(i == 0)
    def _():
        f_ref[0:halo, :] = jnp.zeros((halo, f_ref.shape[1]), F32)

    f_ref[halo:halo + tile, :] = a_ref[...] * jax.nn.sigmoid(gate_ref[...])
    first = halo - (width - 1)
    for c in range(tile // chunk):
        base = first + c * chunk
        y = jnp.zeros((chunk, f_ref.shape[1]), F32) + cb_ref[...]
        for k in range(width):
            y = y + f_ref[base + k:base + k + chunk, :] * w_ref[k:k + 1, :]
        y = _layer_norm(y, ng_ref[...], nb_ref[...])
        mix_ref[c * chunk:(c + 1) * chunk, :] = y * jax.nn.sigmoid(y)

    @pl.when(i == pl.num_programs(0) - 1)
    def _():
        st_ref[...] = f_ref[halo + tile - (width - 1):halo + tile, :]

    f_ref[0:halo, :] = f_ref[tile:tile + halo, :]


def _conv_prompt(proj, conv_w, conv_b, norm_g, norm_b):
    n = proj.shape[0]
    width, c = conv_w.shape
    tile = _tile(n, 256)
    chunk = _tile(tile, 32)
    vec = lambda v: v.reshape(1, c)
    row = pl.BlockSpec((1, c), lambda i: (0, 0))
    return pl.pallas_call(
        functools.partial(_conv_prompt_kernel, tile=tile, width=width, chunk=chunk),
        grid=(n // tile,),
        in_specs=[pl.BlockSpec((tile, c), lambda i: (i, 0)),
                  pl.BlockSpec((tile, c), lambda i: (i, 1)),
                  pl.BlockSpec((width, c), lambda i: (0, 0)),
                  row, row, row],
        out_specs=[pl.BlockSpec((tile, c), lambda i: (i, 0)),
                   pl.BlockSpec((width - 1, c), lambda i: (0, 0))],
        out_shape=[jax.ShapeDtypeStruct((n, c), F32),
                   jax.ShapeDtypeStruct((width - 1, c), F32)],
        scratch_shapes=[pltpu.VMEM((tile + CONV_HALO, c), F32)],
        compiler_params=_params("arbitrary"),
        name="conv_prompt",
    )(proj, proj, conv_w, vec(conv_b), vec(norm_g), vec(norm_b))


def _conv_sample_kernel(a_ref, gate_ref, past_ref, w_ref, cb_ref, ng_ref, nb_ref,
                        mix_ref, st_ref, *, t_new, width):
    hist = width - 1
    new = [a_ref[t] * jax.nn.sigmoid(gate_ref[t]) for t in range(t_new)]
    full = lambda j: past_ref[j] if j < hist else new[j - hist]
    for t in range(t_new):
        y = jnp.zeros(new[0].shape, F32) + cb_ref[...]
        for k in range(width):
            y = y + full(t + k) * w_ref[k:k + 1, :]
        y = _layer_norm(y, ng_ref[...], nb_ref[...])
        mix_ref[t] = y * jax.nn.sigmoid(y)
    for j in range(hist):
        st_ref[j] = full(j + t_new)


def _conv_sample(proj_tm, past_tm, conv_w, conv_b, norm_g, norm_b):
    t_new, batch, _ = proj_tm.shape
    width, c = conv_w.shape
    vec = lambda v: v.reshape(1, c)
    row = pl.BlockSpec((1, c), lambda i: (0, 0))
    return pl.pallas_call(
        functools.partial(_conv_sample_kernel, t_new=t_new, width=width),
        grid=(1,),
        in_specs=[pl.BlockSpec((t_new, batch, c), lambda i: (0, 0, 0)),
                  pl.BlockSpec((t_new, batch, c), lambda i: (0, 0, 1)),
                  pl.BlockSpec((width - 1, batch, c), lambda i: (0, 0, 0)),
                  pl.BlockSpec((width, c), lambda i: (0, 0)),
                  row, row, row],
        out_specs=[pl.BlockSpec((t_new, batch, c), lambda i: (0, 0, 0)),
                   pl.BlockSpec((width - 1, batch, c), lambda i: (0, 0, 0))],
        out_shape=[jax.ShapeDtypeStruct((t_new, batch, c), F32),
                   jax.ShapeDtypeStruct((width - 1, batch, c), F32)],
        compiler_params=_params("arbitrary"),
        name="conv_sample",
    )(proj_tm, proj_tm, past_tm, conv_w, vec(conv_b), vec(norm_g), vec(norm_b))


def _mem_attn_kernel(q_ref, kt_ref, vt_ref, o_ref, *, heads, scale):
    q = q_ref[0]
    kt = kt_ref[0].astype(BF16)
    vt = vt_ref[0].astype(BF16)
    width = q.shape[-1]
    head_of_lane = lax.broadcasted_iota(jnp.int32, (1, width), 1) // (width // heads)
    out = jnp.zeros(q.shape, F32)
    for h in range(heads):
        sel = head_of_lane == h
        s = _dot(jnp.where(sel, q, 0.0).astype(BF16), kt) * scale
        p = jnp.exp(s - jnp.max(s, axis=-1, keepdims=True))
        p = p / jnp.sum(p, axis=-1, keepdims=True)
        out = out + jnp.where(sel, _dot_nt(p.astype(BF16), vt), 0.0)
    o_ref[0] = out


def _mem_attn(q_src, col_block, mem_kt, mem_vt, heads):
    b, t, _ = q_src.shape
    width, m = mem_kt.shape[1:]
    tq = _tile(t, 512)
    scale = float(width // heads) ** -0.5
    return pl.pallas_call(
        functools.partial(_mem_attn_kernel, heads=heads, scale=scale),
        grid=(b, t // tq),
        in_specs=[pl.BlockSpec((1, tq, width), lambda i, j: (i, j, col_block)),
                  pl.BlockSpec((1, width, m), lambda i, j: (i, 0, 0)),
                  pl.BlockSpec((1, width, m), lambda i, j: (i, 0, 0))],
        out_specs=pl.BlockSpec((1, tq, width), lambda i, j: (i, j, 0)),
        out_shape=jax.ShapeDtypeStruct((b, t, width), F32),
        compiler_params=_params("parallel", "parallel"),
        name="mem_attn",
    )(q_src, mem_kt, mem_vt)


def _out_proj_kernel(x_ref, mix_ref, mem_ref, wa_ref, wb_ref, b_ref, g_ref, beta_ref,
                     o_ref, *, alpha):
    acc = _dot(mix_ref[...].astype(BF16), wa_ref[...])
    acc = acc + _dot(mem_ref[...].astype(BF16), wb_ref[...]) + b_ref[...]
    o_ref[...] = _layer_norm(alpha * x_ref[...] + acc, g_ref[...], beta_ref[...])


def _out_proj(x, mix, mem, w_out_bf, b_out, g, beta, alpha):
    m, d = x.shape
    cm = mix.shape[1]
    cw = mem.shape[1]
    tm = _tile(m, 256)
    vec = lambda v: v.reshape(1, d)
    row = pl.BlockSpec((1, d), lambda i: (0, 0))
    return pl.pallas_call(
        functools.partial(_out_proj_kernel, alpha=alpha),
        grid=(m // tm,),
        in_specs=[pl.BlockSpec((tm, d), lambda i: (i, 0)),
                  pl.BlockSpec((tm, cm), lambda i: (i, 0)),
                  pl.BlockSpec((tm, cw), lambda i: (i, 0)),
                  pl.BlockSpec((cm, d), lambda i: (0, 0)),
                  pl.BlockSpec((cw, d), lambda i: (0, 0)),
                  row, row, row],
        out_specs=pl.BlockSpec((tm, d), lambda i: (i, 0)),
        out_shape=jax.ShapeDtypeStruct((m, d), F32),
        compiler_params=_params("parallel"),
        name="out_proj_ln",
    )(x, mix, mem, w_out_bf[:cm], w_out_bf[cm:], vec(b_out), vec(g), vec(beta))


def _top2(rows):
    best, bi = rows[0], jnp.zeros(rows[0].shape, jnp.int32)
    for j in range(1, len(rows)):
        upd = rows[j] > best
        best = jnp.where(upd, rows[j], best)
        bi = jnp.where(upd, j, bi)
    sec, si = None, None
    for j in range(len(rows)):
        cand = jnp.where(bi == j, -jnp.inf, rows[j])
        if sec is None:
            sec, si = cand, jnp.zeros(cand.shape, jnp.int32)
        else:
            upd = cand > sec
            sec = jnp.where(upd, cand, sec)
            si = jnp.where(upd, j, si)
    return best, sec, bi, si


def _router_kernel(x_ref, wt_ref, b_ref, o_ref, *, n_groups):
    logits = lax.dot_general(wt_ref[...], x_ref[...], (((1,), (1,)), ((), ())),
                             preferred_element_type=F32,
                             precision=lax.Precision.HIGHEST) + b_ref[...]
    n_exp = logits.shape[0]
    per = n_exp // n_groups
    e = jnp.exp(logits - jnp.max(logits, axis=0, keepdims=True))
    probs = e / jnp.sum(e, axis=0, keepdims=True)
    picks = [_top2([probs[g * per + j:g * per + j + 1, :] for j in range(per)])
             for g in range(n_groups)]
    score = picks[0][0] + picks[0][1]
    sel = picks[0]
    g_sel = jnp.zeros(score.shape, jnp.int32)
    for g in range(1, n_groups):
        sc = picks[g][0] + picks[g][1]
        upd = sc > score
        score = jnp.where(upd, sc, score)
        g_sel = jnp.where(upd, g, g_sel)
        sel = tuple(jnp.where(upd, a, b) for a, b in zip(picks[g], sel))
    p1, p2, i1, i2 = sel
    denom = p1 + p2
    e1 = g_sel * per + i1
    e2 = g_sel * per + i2
    eidx = lax.broadcasted_iota(jnp.int32, logits.shape, 0)
    o_ref[...] = (jnp.where(eidx == e1, p1 / denom, 0.0)
                  + jnp.where(eidx == e2, p2 / denom, 0.0))


def _router(x, w_router, b_router):
    m, d = x.shape
    n_exp = w_router.shape[1]
    tm = _tile(m, 512)
    return pl.pallas_call(
        functools.partial(_router_kernel, n_groups=N_GROUPS),
        grid=(m // tm,),
        in_specs=[pl.BlockSpec((tm, d), lambda i: (i, 0)),
                  pl.BlockSpec((n_exp, d), lambda i: (0, 0)),
                  pl.BlockSpec((n_exp, 1), lambda i: (0, 0))],
        out_specs=pl.BlockSpec((n_exp, tm), lambda i: (0, i)),
        out_shape=jax.ShapeDtypeStruct((n_exp, m), F32),
        compiler_params=_params("parallel"),
        name="router",
    )(x, w_router.T, b_router.reshape(n_exp, 1))


def _moe_kernel(x_ref, gates_ref, wg_ref, wu_ref, wd_ref, g_ref, beta_ref, o_ref,
                xb_ref, acc_ref, *, alpha):
    e = pl.program_id(1)

    @pl.when(e == 0)
    def _():
        xb_ref[...] = x_ref[...].astype(BF16)
        acc_ref[...] = jnp.zeros(acc_ref.shape, F32)

    gates = gates_ref[...]
    lane = lax.broadcasted_iota(jnp.int32, gates.shape, 1)
    gate = jnp.sum(jnp.where(lane == e, gates, 0.0), axis=1, keepdims=True)
    xb = xb_ref[...]
    hg = _dot(xb, wg_ref[0])
    hu = _dot(xb, wu_ref[0])
    h = hg * jax.nn.sigmoid(hg) * hu * gate
    acc_ref[...] += _dot(h.astype(BF16), wd_ref[0])

    @pl.when(e == pl.num_programs(1) - 1)
    def _():
        o_ref[...] = _layer_norm(alpha * x_ref[...] + acc_ref[...], g_ref[...], beta_ref[...])


def _moe(x, gates, wg_bf, wu_bf, wd_bf, g, beta, alpha):
    m, d = x.shape
    n_exp, _, ff = wg_bf.shape
    tm = _tile(m, 512)
    vec = lambda v: v.reshape(1, d)
    row = pl.BlockSpec((1, d), lambda i, e: (0, 0))
    return pl.pallas_call(
        functools.partial(_moe_kernel, alpha=alpha),
        grid=(m // tm, n_exp),
        in_specs=[pl.BlockSpec((tm, d), lambda i, e: (i, 0)),
                  pl.BlockSpec((tm, n_exp), lambda i, e: (i, 0)),
                  pl.BlockSpec((1, d, ff), lambda i, e: (e, 0, 0)),
                  pl.BlockSpec((1, d, ff), lambda i, e: (e, 0, 0)),
                  pl.BlockSpec((1, ff, d), lambda i, e: (e, 0, 0)),
                  row, row],
        out_specs=pl.BlockSpec((tm, d), lambda i, e: (i, 0)),
        out_shape=jax.ShapeDtypeStruct((m, d), F32),
        scratch_shapes=[pltpu.VMEM((tm, d), BF16), pltpu.VMEM((tm, d), F32)],
        compiler_params=_params("parallel", "arbitrary"),
        name="moe",
    )(x, gates, wg_bf, wu_bf, wd_bf, vec(g), vec(beta))


def _sb_weights(zs, u_neg, mask):
    neg_log_keep, log_beta = [], []
    for z in zs:
        neg_abs = pltpu.bitcast(pltpu.bitcast(z, jnp.uint32) | jnp.uint32(0x80000000), F32)
        soft = jnp.log(1.0 + jnp.exp2(neg_abs)) * LOG2E
        lb = jnp.minimum(z, 0.0) - soft
        nlk = z - lb
        if mask is not None:
            nlk = jnp.where(mask, nlk, 0.0)
        neg_log_keep.append(nlk.astype(BF16))
        log_beta.append(lb)
    later = [_dot(nlk, u_neg) for nlk in neg_log_keep]
    out = []
    for nlk, lb, lt in zip(neg_log_keep, log_beta, later):
        w = jnp.exp2(lb + lt)
        if mask is not None:
            w = jnp.where(mask, w, 0.0)
        out.append((w.astype(BF16), lt[:, 0:1] - nlk[:, 0:1].astype(F32)))
    return out


def _u_neg(size):
    r = lax.broadcasted_iota(jnp.int32, (size, size), 0)
    c = lax.broadcasted_iota(jnp.int32, (size, size), 1)
    return jnp.where(r > c, -1.0, 0.0).astype(BF16)


def _sb_prompt_kernel(bias_ref, q_ref, kt_ref, vt_ref, o_ref, acc_ref, s_ref,
                      *, blk, head_dim):
    hp = pl.program_id(0)
    i = pl.program_id(1)
    heads_per = q_ref.shape[1] // head_dim
    q2 = q_ref[...]
    head_of_lane = lax.broadcasted_iota(jnp.int32, (1, q2.shape[1]), 1) // head_dim
    u_neg = _u_neg(blk)
    qs = [jnp.where(head_of_lane == h, q2, jnp.zeros_like(q2)) for h in range(heads_per)]
    biases = [bias_ref[hp * heads_per + h] for h in range(heads_per)]

    def visit(key_blocks, mask, first):
        offs = [pl.multiple_of(kb * blk, blk) for kb in key_blocks]
        units = [(o, h) for o in offs for h in range(heads_per)]
        zs = [_dot(qs[h], kt_ref[:, pl.ds(o, blk)]) + biases[h] for o, h in units]
        wt = _sb_weights(zs, u_neg, mask)
        pvs = [_dot_nt(w, vt_ref[:, pl.ds(o, blk)]) for (w, _), (o, h) in zip(wt, units)]
        for (_, total), pv, (o, h) in zip(wt, pvs, units):
            if first:
                acc_ref[h] = pv
                s_ref[h] = total
            else:
                acc_ref[h] += jnp.exp2(s_ref[h]) * pv
                s_ref[h] += total

    r = lax.broadcasted_iota(jnp.int32, (blk, blk), 0)
    c = lax.broadcasted_iota(jnp.int32, (blk, blk), 1)
    visit([i], c < r, True)

    @pl.when(i % 2 == 1)
    def _():
        visit([i - 1], None, False)

    def body(step, carry):
        top = i - (i % 2) - 2 * step
        visit([top - 1, top - 2], None, False)
        return carry

    lax.fori_loop(0, i // 2, body, 0)
    out = acc_ref[0]
    for h in range(1, heads_per):
        out = jnp.where(head_of_lane == h, acc_ref[h], out)
    o_ref[...] = out


def _sb_prompt(q_bf, kt_bf, vt_bf, bias2, head_dim):
    n, width = q_bf.shape
    n_groups = width // LANES
    blk = _tile(n, 256)
    return pl.pallas_call(
        functools.partial(_sb_prompt_kernel, blk=blk, head_dim=head_dim),
        grid=(n_groups, n // blk),
        in_specs=[pl.BlockSpec(memory_space=pltpu.SMEM),
                  pl.BlockSpec((blk, LANES), lambda p, i: (i, p)),
                  pl.BlockSpec((LANES, n), lambda p, i: (p, 0)),
                  pl.BlockSpec((LANES, n), lambda p, i: (p, 0))],
        out_specs=pl.BlockSpec((blk, LANES), lambda p, i: (i, p)),
        out_shape=jax.ShapeDtypeStruct((n, width), F32),
        scratch_shapes=[pltpu.VMEM((LANES // head_dim, blk, LANES), F32),
                        pltpu.VMEM((LANES // head_dim, blk, 1), F32)],
        compiler_params=_params("parallel", "arbitrary"),
        name="sb_prompt",
    )(bias2, q_bf, kt_bf, vt_bf)


def _sb_sample_kernel(pt_ref, qbd_ref, knew_ref, vnew_ref, bias_ref, kpool_ref, vpool_ref,
                      o_ref, kbuf, vbuf, sem, pad_k, pad_v, acc_ref,
                      *, n_pages, page, group, sub, t_new):
    b = pl.program_id(0)
    n_steps = n_pages // group

    def copies(step, slot):
        out = []
        for r in range(group):
            phys = pt_ref[b, (n_steps - 1 - step) * group + r]
            dst = pl.ds(r * page, page)
            out.append(pltpu.make_async_copy(kpool_ref.at[phys], kbuf.at[slot, :, dst],
                                             sem.at[0, slot]))
            out.append(pltpu.make_async_copy(vpool_ref.at[phys], vbuf.at[slot, :, dst],
                                             sem.at[1, slot]))
        return out

    for cp in copies(0, 0):
        cp.start()

    qbd = qbd_ref[0]
    bias = bias_ref[...]
    rows = qbd.shape[0]

    pad_k[...] = jnp.zeros(pad_k.shape, F32)
    pad_v[...] = jnp.zeros(pad_v.shape, F32)
    pad_k[0:t_new, :] = knew_ref[0]
    pad_v[0:t_new, :] = vnew_ref[0]
    t_of_row = lax.broadcasted_iota(jnp.int32, (rows, page), 0) % SB_ROWS_PER_HEAD
    s_of_col = lax.broadcasted_iota(jnp.int32, (rows, page), 1)
    (w, s_after), = _sb_weights([_dot_nt(qbd, pad_k[...].astype(BF16)) + bias], _u_neg(page),
                                s_of_col < t_of_row)
    acc_ref[...] = _dot(w, pad_v[...].astype(BF16))

    u_neg = _u_neg(sub)

    def body(step, s_after):
        slot = step % 2
        for cp in copies(step, slot):
            cp.wait()

        @pl.when(step + 1 < n_steps)
        def _():
            for cp in copies(step + 1, 1 - slot):
                cp.start()

        starts = list(reversed(range(0, group * page, sub)))
        zs = [_dot(qbd, kbuf[slot, :, a:a + sub].astype(BF16)) + bias for a in starts]
        wt = _sb_weights(zs, u_neg, None)
        pvs = [_dot_nt(w, vbuf[slot, :, a:a + sub].astype(BF16)) for (w, _), a in zip(wt, starts)]
        upd = jnp.zeros(acc_ref.shape, F32)
        for (_, total), pv in zip(wt, pvs):
            upd = upd + jnp.exp2(s_after) * pv
            s_after = s_after + total
        acc_ref[...] += upd
        return s_after

    lax.fori_loop(0, n_steps, body, s_after)

    width = acc_ref.shape[1]
    n_heads = rows // SB_ROWS_PER_HEAD
    head_dim = width // n_heads
    head_of_row = lax.broadcasted_iota(jnp.int32, (rows, width), 0) // SB_ROWS_PER_HEAD
    head_of_col = lax.broadcasted_iota(jnp.int32, (rows, width), 1) // head_dim
    own = jnp.where(head_of_row == head_of_col, acc_ref[...], 0.0)
    out = own[0:SB_ROWS_PER_HEAD, :]
    for h in range(1, n_heads):
        out = out + own[h * SB_ROWS_PER_HEAD:(h + 1) * SB_ROWS_PER_HEAD, :]
    o_ref[0] = out[0:t_new, :]


def _sb_sample(q, k_new, v_new, bias2, pool_kt, pool_vt, page_table, q_scale):
    b, t_new, width = q.shape
    n_heads = bias2.shape[0]
    head_dim = width // n_heads
    page = pool_kt.shape[2]
    n_pages = page_table.shape[1]
    group = max(g for g in (4, 2, 1) if n_pages % g == 0)
    sub = min(2, group) * page
    rows = n_heads * SB_ROWS_PER_HEAD
    head_of_col = jnp.arange(width) // head_dim
    qpad = jnp.pad(q * q_scale, ((0, 0), (0, SB_ROWS_PER_HEAD - t_new), (0, 0)))
    qbd = jnp.where(head_of_col[None, None, None, :] == jnp.arange(n_heads)[None, :, None, None],
                    qpad[:, None, :, :], 0.0).reshape(b, rows, width).astype(BF16)
    bias_rows = jnp.repeat(bias2, SB_ROWS_PER_HEAD).reshape(rows, 1)
    return pl.pallas_call(
        functools.partial(_sb_sample_kernel, n_pages=n_pages, page=page, group=group, sub=sub,
                          t_new=t_new),
        grid_spec=pltpu.PrefetchScalarGridSpec(
            num_scalar_prefetch=1,
            grid=(b,),
            in_specs=[pl.BlockSpec((1, rows, width), lambda i, pt: (i, 0, 0)),
                      pl.BlockSpec((1, t_new, width), lambda i, pt: (i, 0, 0)),
                      pl.BlockSpec((1, t_new, width), lambda i, pt: (i, 0, 0)),
                      pl.BlockSpec((rows, 1), lambda i, pt: (0, 0)),
                      pl.BlockSpec(memory_space=pl.ANY),
                      pl.BlockSpec(memory_space=pl.ANY)],
            out_specs=pl.BlockSpec((1, t_new, width), lambda i, pt: (i, 0, 0)),
            scratch_shapes=[pltpu.VMEM((2, width, group * page), F32),
                            pltpu.VMEM((2, width, group * page), F32),
                            pltpu.SemaphoreType.DMA((2, 2)),
                            pltpu.VMEM((page, width), F32),
                            pltpu.VMEM((page, width), F32),
                            pltpu.VMEM((rows, width), F32)]),
        out_shape=jax.ShapeDtypeStruct((b, t_new, width), F32),
        compiler_params=_params("arbitrary"),
        name="sb_sample",
    )(page_table, qbd, k_new, v_new, bias_rows, pool_kt, pool_vt)


def _feature_major(t):
    lead = t.shape[:-3]
    tokens, heads, dim = t.shape[-3:]
    n = len(lead)
    return jnp.transpose(t, (*range(n), n + 1, n + 2, n)).reshape(*lead, heads * dim, tokens)


def _token_major(t, heads):
    lead = t.shape[:-2]
    width, tokens = t.shape[-2:]
    n = len(lead)
    t = t.reshape(*lead, heads, width // heads, tokens)
    return jnp.transpose(t, (*range(n), n + 2, n, n + 1))


def kernel(x_prompt, x_sample, state_conv, cache_sb_k, cache_sb_v, cache_mem_k, cache_mem_v, page_table, mem_prompt, a_w_in, a_b_in, a_conv_w, a_conv_b, a_norm_g, a_norm_b, a_w_out, a_b_out, b_w_in, b_b_in, b_sb_bias, b_w_out, b_b_out, w_mem_kv, ln_mix_g, ln_mix_b, ln_ffn_g, ln_ffn_b, w_router, b_router, w_exp_gate, w_exp_up, w_exp_down):
    depth = w_mem_kv.shape[0]
    n_mixers = 2
    alpha = (2.0 * depth) ** 0.25
    bp, seq, d = x_prompt.shape
    bs, t_new, _ = x_sample.shape
    mem_heads, mem_hd = cache_mem_k.shape[-2:]
    mem_w = mem_heads * mem_hd
    sb_heads, sb_hd = cache_sb_k.shape[-2:]
    sb_w = sb_heads * sb_hd
    conv_ch = a_conv_w.shape[-1]
    mem_tokens = mem_prompt.shape[1]
    assert bp == 1
    sb_q_scale = float(sb_hd) ** -0.5 * LOG2E

    y_p = x_prompt.reshape(seq, d)
    y_s = x_sample.reshape(bs * t_new, d)
    mem_p = mem_prompt.reshape(mem_tokens, d)
    conv_p, conv_s, sbk_p, sbv_p, sbk_s, sbv_s, memk_p, memv_p = ([] for _ in range(8))

    for layer in range(depth):
        kvt = _linear_t(mem_p, w_mem_kv[layer].T.astype(BF16), name="mem_kv")
        mkt_p = kvt[:mem_w][None]
        mvt_p = kvt[mem_w:][None]
        memk_p.append(_token_major(mkt_p, mem_heads))
        memv_p.append(_token_major(mvt_p, mem_heads))
        mkt_s = _feature_major(cache_mem_k[layer])
        mvt_s = _feature_major(cache_mem_v[layer])
        i = layer // n_mixers
        if layer % n_mixers == 0:
            w_in = a_w_in[i].astype(BF16)
            proj_p = _linear(y_p, w_in, a_b_in[i], name="conv_in_proj")
            proj_s = _linear(y_s, w_in, a_b_in[i], name="conv_in_proj_s")
            mix_p, st_p = _conv_prompt(proj_p, a_conv_w[i], a_conv_b[i], a_norm_g[i], a_norm_b[i])
            conv_p.append(st_p[None])
            proj_tm = jnp.transpose(proj_s.reshape(bs, t_new, -1), (1, 0, 2))
            mix_s, st_s = _conv_sample(proj_tm, jnp.transpose(state_conv[i], (1, 0, 2)),
                                       a_conv_w[i], a_conv_b[i], a_norm_g[i], a_norm_b[i])
            mix_s = jnp.transpose(mix_s, (1, 0, 2)).reshape(bs * t_new, conv_ch)
            conv_s.append(jnp.transpose(st_s, (1, 0, 2)))
            q_col = (2 * conv_ch) // mem_w
            assert q_col * mem_w == 2 * conv_ch
            qm_p, qm_s = proj_p[None], proj_s.reshape(bs, t_new, -1)
            w_out, b_out = a_w_out[i], a_b_out[i]
        else:
            bias2 = b_sb_bias[i] * LOG2E
            q_bf, kt, vt, kt_bf, vt_bf, qm = _sb_in_proj(y_p, b_w_in[i], b_b_in[i], sb_w, sb_q_scale)
            sbk_p.append(_token_major(kt[None], sb_heads))
            sbv_p.append(_token_major(vt[None], sb_heads))
            mix_p = _sb_prompt(q_bf, kt_bf, vt_bf, bias2, sb_hd)
            proj_s = _linear(y_s, b_w_in[i].astype(BF16), b_b_in[i], name="sb_in_proj_s")
            proj_s3 = proj_s.reshape(bs, t_new, -1)
            k_s = proj_s3[:, :, sb_w:2 * sb_w]
            v_s = proj_s3[:, :, 2 * sb_w:3 * sb_w]
            sbk_s.append(k_s.reshape(bs, t_new, sb_heads, sb_hd))
            sbv_s.append(v_s.reshape(bs, t_new, sb_heads, sb_hd))
            mix_s = _sb_sample(proj_s3[:, :, :sb_w], k_s, v_s, bias2,
                               _feature_major(cache_sb_k[i]), _feature_major(cache_sb_v[i]),
                               page_table, sb_q_scale)
            mix_s = mix_s.reshape(bs * t_new, sb_w)
            q_col = (3 * sb_w) // mem_w
            assert q_col * mem_w == 3 * sb_w
            qm_p, qm_s = qm[None], proj_s3
            q_col_p = 0
            w_out, b_out = b_w_out[i], b_b_out[i]

        mem_out_p = _mem_attn(qm_p, q_col if layer % n_mixers == 0 else q_col_p, mkt_p, mvt_p,
                              mem_heads)
        mem_out_s = _mem_attn(qm_s, q_col, mkt_s, mvt_s, mem_heads)
        w_out_bf = w_out.astype(BF16)
        y_p = _out_proj(y_p, mix_p, mem_out_p.reshape(seq, mem_w), w_out_bf, b_out,
                        ln_mix_g[layer], ln_mix_b[layer], alpha)
        y_s = _out_proj(y_s, mix_s, mem_out_s.reshape(bs * t_new, mem_w), w_out_bf, b_out,
                        ln_mix_g[layer], ln_mix_b[layer], alpha)

        wg, wu, wd = (w[layer].astype(BF16) for w in (w_exp_gate, w_exp_up, w_exp_down))
        y_p = _moe(y_p, _router(y_p, w_router, b_router).T, wg, wu, wd,
                   ln_ffn_g[layer], ln_ffn_b[layer], alpha)
        y_s = _moe(y_s, _router(y_s, w_router, b_router).T, wg, wu, wd,
                   ln_ffn_g[layer], ln_ffn_b[layer], alpha)

    return (y_p.reshape(bp, seq, d), y_s.reshape(bs, t_new, d),
            jnp.stack(conv_p), jnp.stack(conv_s), jnp.stack(sbk_p), jnp.stack(sbv_p),
            jnp.stack(sbk_s), jnp.stack(sbv_s), jnp.stack(memk_p), jnp.stack(memv_p))
```

```python
import functools
import math

import jax
import jax.numpy as jnp
from jax import lax
from jax.experimental import pallas as pl
from jax.experimental.pallas import tpu as pltpu

LN_EPS = 1e-5
N_GROUPS = 4
TOP_K = 2
CONV_HALO = 32
LANES = 128
SB_ROWS_PER_HEAD = 8
SB_KEY_BLOCKS_PER_STEP = 4
SB_PAGE_SLOTS = 3
MOE_CHUNK = 1024
VMEM_LIMIT = 48 * 1024 * 1024
LOG2E = math.log2(math.e)

BF16 = jnp.bfloat16
F32 = jnp.float32


def _tile(n, pref):
    if n <= pref:
        return n
    t = pref - pref % 8
    while t >= 8:
        if n % t == 0:
            return t
        t -= 8
    return n


def _params(*sem):
    return pltpu.CompilerParams(dimension_semantics=sem, vmem_limit_bytes=VMEM_LIMIT)


def _layer_norm(x, g, b):
    mu = jnp.mean(x, axis=-1, keepdims=True)
    xc = x - mu
    var = jnp.mean(xc * xc, axis=-1, keepdims=True)
    return xc * lax.rsqrt(var + LN_EPS) * g + b


def _dot(a, b):
    return jnp.dot(a, b, preferred_element_type=F32)


def _dot_nt(a, b):
    return lax.dot_general(a, b, (((1,), (1,)), ((), ())), preferred_element_type=F32)


def _linear_kernel(x_ref, w_ref, b_ref, o_ref):
    o_ref[...] = _dot(x_ref[...].astype(BF16), w_ref[...]) + b_ref[...]


def _linear(x, w_bf, b, *, name="linear"):
    m, k = x.shape
    n = w_bf.shape[1]
    tm = _tile(m, 256)
    return pl.pallas_call(
        _linear_kernel,
        grid=(m // tm,),
        in_specs=[pl.BlockSpec((tm, k), lambda i: (i, 0)),
                  pl.BlockSpec((k, n), lambda i: (0, 0)),
                  pl.BlockSpec((1, n), lambda i: (0, 0))],
        out_specs=pl.BlockSpec((tm, n), lambda i: (i, 0)),
        out_shape=jax.ShapeDtypeStruct((m, n), F32),
        compiler_params=_params("parallel"),
        name=name,
    )(x, w_bf, b.reshape(1, n))


def _linear_t_kernel(x_ref, wt_ref, o_ref):
    o_ref[...] = _dot_nt(wt_ref[...], x_ref[...].astype(BF16))


def _linear_t(x, wt_bf, *, name="linear_t"):
    m, k = x.shape
    n = wt_bf.shape[0]
    tm = _tile(m, 256)
    return pl.pallas_call(
        _linear_t_kernel,
        grid=(m // tm,),
        in_specs=[pl.BlockSpec((tm, k), lambda i: (i, 0)),
                  pl.BlockSpec((n, k), lambda i: (0, 0))],
        out_specs=pl.BlockSpec((n, tm), lambda i: (0, i)),
        out_shape=jax.ShapeDtypeStruct((n, m), F32),
        compiler_params=_params("parallel"),
        name=name,
    )(x, wt_bf)


def _sb_in_proj_kernel(x_ref, wq_ref, wkt_ref, wvt_ref, wm_ref, bq_ref, bk_ref, bv_ref, bm_ref,
                       q_ref, kt_ref, vt_ref, ktb_ref, vtb_ref, qm_ref, *, q_scale):
    xb = x_ref[...].astype(BF16)
    q_ref[...] = ((_dot(xb, wq_ref[...]) + bq_ref[...]) * q_scale).astype(BF16)
    kt = _dot_nt(wkt_ref[...], xb) + bk_ref[...]
    kt_ref[...] = kt
    ktb_ref[...] = kt.astype(BF16)
    vt = _dot_nt(wvt_ref[...], xb) + bv_ref[...]
    vt_ref[...] = vt
    vtb_ref[...] = vt.astype(BF16)
    qm_ref[...] = _dot(xb, wm_ref[...]) + bm_ref[...]


def _sb_in_proj(x, w_in, b_in, sb_w, q_scale):
    n, d = x.shape
    wm = w_in.shape[1] - 3 * sb_w
    tm = _tile(n, 256)
    wq = w_in[:, :sb_w].astype(BF16)
    wkt = w_in[:, sb_w:2 * sb_w].T.astype(BF16)
    wvt = w_in[:, 2 * sb_w:3 * sb_w].T.astype(BF16)
    wmem = w_in[:, 3 * sb_w:].astype(BF16)
    const = lambda shape: pl.BlockSpec(shape, lambda i: (0, 0))
    return pl.pallas_call(
        functools.partial(_sb_in_proj_kernel, q_scale=q_scale),
        grid=(n // tm,),
        in_specs=[pl.BlockSpec((tm, d), lambda i: (i, 0)),
                  const((d, sb_w)), const((sb_w, d)), const((sb_w, d)), const((d, wm)),
                  const((1, sb_w)), const((sb_w, 1)), const((sb_w, 1)), const((1, wm))],
        out_specs=[pl.BlockSpec((tm, sb_w), lambda i: (i, 0)),
                   pl.BlockSpec((sb_w, tm), lambda i: (0, i)),
                   pl.BlockSpec((sb_w, tm), lambda i: (0, i)),
                   pl.BlockSpec((sb_w, tm), lambda i: (0, i)),
                   pl.BlockSpec((sb_w, tm), lambda i: (0, i)),
                   pl.BlockSpec((tm, wm), lambda i: (i, 0))],
        out_shape=[jax.ShapeDtypeStruct((n, sb_w), BF16),
                   jax.ShapeDtypeStruct((sb_w, n), F32),
                   jax.ShapeDtypeStruct((sb_w, n), F32),
                   jax.ShapeDtypeStruct((sb_w, n), BF16),
                   jax.ShapeDtypeStruct((sb_w, n), BF16),
                   jax.ShapeDtypeStruct((n, wm), F32)],
        compiler_params=_params("parallel"),
        name="sb_in_proj",
    )(x, wq, wkt, wvt, wmem,
      b_in[:sb_w].reshape(1, sb_w), b_in[sb_w:2 * sb_w].reshape(sb_w, 1),
      b_in[2 * sb_w:3 * sb_w].reshape(sb_w, 1), b_in[3 * sb_w:].reshape(1, wm))


def _conv_prompt_kernel(a_ref, gate_ref, w_ref, cb_ref, ng_ref, nb_ref,
                        mix_ref, st_ref, f_ref, *, tile, width, chunk):
    i = pl.program_id(0)
    halo = CONV_HALO

    @pl.when(i == 0)
    def _():
        f_ref[0:halo, :] = jnp.zeros((halo, f_ref.shape[1]), F32)

    f_ref[halo:halo + tile, :] = a_ref[...] * jax.nn.sigmoid(gate_ref[...])
    first = halo - (width - 1)
    for c in range(tile // chunk):
        base = first + c * chunk
        y = jnp.zeros((chunk, f_ref.shape[1]), F32) + cb_ref[...]
        for k in range(width):
            y = y + f_ref[base + k:base + k + chunk, :] * w_ref[k:k + 1, :]
        y = _layer_norm(y, ng_ref[...], nb_ref[...])
        mix_ref[c * chunk:(c + 1) * chunk, :] = y * jax.nn.sigmoid(y)

    @pl.when(i == pl.num_programs(0) - 1)
    def _():
        st_ref[...] = f_ref[halo + tile - (width - 1):halo + tile, :]

    f_ref[0:halo, :] = f_ref[tile:tile + halo, :]


def _conv_prompt(proj, conv_w, conv_b, norm_g, norm_b):
    n = proj.shape[0]
    width, c = conv_w.shape
    tile = _tile(n, 256)
    chunk = _tile(tile, 32)
    vec = lambda v: v.reshape(1, c)
    row = pl.BlockSpec((1, c), lambda i: (0, 0))
    return pl.pallas_call(
        functools.partial(_conv_prompt_kernel, tile=tile, width=width, chunk=chunk),
        grid=(n // tile,),
        in_specs=[pl.BlockSpec((tile, c), lambda i: (i, 0)),
                  pl.BlockSpec((tile, c), lambda i: (i, 1)),
                  pl.BlockSpec((width, c), lambda i: (0, 0)),
                  row, row, row],
        out_specs=[pl.BlockSpec((tile, c), lambda i: (i, 0)),
                   pl.BlockSpec((width - 1, c), lambda i: (0, 0))],
        out_shape=[jax.ShapeDtypeStruct((n, c), F32),
                   jax.ShapeDtypeStruct((width - 1, c), F32)],
        scratch_shapes=[pltpu.VMEM((tile + CONV_HALO, c), F32)],
        compiler_params=_params("arbitrary"),
        name="conv_prompt",
    )(proj, proj, conv_w, vec(conv_b), vec(norm_g), vec(norm_b))


def _conv_sample_kernel(a_ref, gate_ref, past_ref, w_ref, cb_ref, ng_ref, nb_ref,
                        mix_ref, st_ref, *, t_new, width):
    hist = width - 1
    new = [a_ref[t] * jax.nn.sigmoid(gate_ref[t]) for t in range(t_new)]
    full = lambda j: past_ref[j] if j < hist else new[j - hist]
    for t in range(t_new):
        y = jnp.zeros(new[0].shape, F32) + cb_ref[...]
        for k in range(width):
            y = y + full(t + k) * w_ref[k:k + 1, :]
        y = _layer_norm(y, ng_ref[...], nb_ref[...])
        mix_ref[t] = y * jax.nn.sigmoid(y)
    for j in range(hist):
        st_ref[j] = full(j + t_new)


def _conv_sample(proj_tm, past_tm, conv_w, conv_b, norm_g, norm_b):
    t_new, batch, _ = proj_tm.shape
    width, c = conv_w.shape
    vec = lambda v: v.reshape(1, c)
    row = pl.BlockSpec((1, c), lambda i: (0, 0))
    return pl.pallas_call(
        functools.partial(_conv_sample_kernel, t_new=t_new, width=width),
        grid=(1,),
        in_specs=[pl.BlockSpec((t_new, batch, c), lambda i: (0, 0, 0)),
                  pl.BlockSpec((t_new, batch, c), lambda i: (0, 0, 1)),
                  pl.BlockSpec((width - 1, batch, c), lambda i: (0, 0, 0)),
                  pl.BlockSpec((width, c), lambda i: (0, 0)),
                  row, row, row],
        out_specs=[pl.BlockSpec((t_new, batch, c), lambda i: (0, 0, 0)),
                   pl.BlockSpec((width - 1, batch, c), lambda i: (0, 0, 0))],
        out_shape=[jax.ShapeDtypeStruct((t_new, batch, c), F32),
                   jax.ShapeDtypeStruct((width - 1, batch, c), F32)],
        compiler_params=_params("arbitrary"),
        name="conv_sample",
    )(proj_tm, proj_tm, past_tm, conv_w, vec(conv_b), vec(norm_g), vec(norm_b))


def _mem_attn_kernel(q_ref, kt_ref, vt_ref, o_ref, *, heads, scale):
    q = q_ref[0]
    kt = kt_ref[0].astype(BF16)
    vt = vt_ref[0].astype(BF16)
    width = q.shape[-1]
    head_of_lane = lax.broadcasted_iota(jnp.int32, (1, width), 1) // (width // heads)
    out = jnp.zeros(q.shape, F32)
    for h in range(heads):
        sel = head_of_lane == h
        s = _dot(jnp.where(sel, q, 0.0).astype(BF16), kt) * scale
        p = jnp.exp(s - jnp.max(s, axis=-1, keepdims=True))
        p = p / jnp.sum(p, axis=-1, keepdims=True)
        out = out + jnp.where(sel, _dot_nt(p.astype(BF16), vt), 0.0)
    o_ref[0] = out


def _mem_attn(q_src, col_block, mem_kt, mem_vt, heads):
    b, t, _ = q_src.shape
    width, m = mem_kt.shape[1:]
    tq = _tile(t, 512)
    scale = float(width // heads) ** -0.5
    return pl.pallas_call(
        functools.partial(_mem_attn_kernel, heads=heads, scale=scale),
        grid=(b, t // tq),
        in_specs=[pl.BlockSpec((1, tq, width), lambda i, j: (i, j, col_block)),
                  pl.BlockSpec((1, width, m), lambda i, j: (i, 0, 0)),
                  pl.BlockSpec((1, width, m), lambda i, j: (i, 0, 0))],
        out_specs=pl.BlockSpec((1, tq, width), lambda i, j: (i, j, 0)),
        out_shape=jax.ShapeDtypeStruct((b, t, width), F32),
        compiler_params=_params("parallel", "parallel"),
        name="mem_attn",
    )(q_src, mem_kt, mem_vt)


def _out_proj_kernel(x_ref, mix_ref, mem_ref, wa_ref, wb_ref, b_ref, g_ref, beta_ref,
                     o_ref, *, alpha):
    acc = _dot(mix_ref[...].astype(BF16), wa_ref[...])
    acc = acc + _dot(mem_ref[...].astype(BF16), wb_ref[...]) + b_ref[...]
    o_ref[...] = _layer_norm(alpha * x_ref[...] + acc, g_ref[...], beta_ref[...])


def _out_proj(x, mix, mem, w_out_bf, b_out, g, beta, alpha):
    m, d = x.shape
    cm = mix.shape[1]
    cw = mem.shape[1]
    tm = _tile(m, 256)
    vec = lambda v: v.reshape(1, d)
    row = pl.BlockSpec((1, d), lambda i: (0, 0))
    return pl.pallas_call(
        functools.partial(_out_proj_kernel, alpha=alpha),
        grid=(m // tm,),
        in_specs=[pl.BlockSpec((tm, d), lambda i: (i, 0)),
                  pl.BlockSpec((tm, cm), lambda i: (i, 0)),
                  pl.BlockSpec((tm, cw), lambda i: (i, 0)),
                  pl.BlockSpec((cm, d), lambda i: (0, 0)),
                  pl.BlockSpec((cw, d), lambda i: (0, 0)),
                  row, row, row],
        out_specs=pl.BlockSpec((tm, d), lambda i: (i, 0)),
        out_shape=jax.ShapeDtypeStruct((m, d), F32),
        compiler_params=_params("parallel"),
        name="out_proj_ln",
    )(x, mix, mem, w_out_bf[:cm], w_out_bf[cm:], vec(b_out), vec(g), vec(beta))


def _top2(rows):
    best, bi = rows[0], jnp.zeros(rows[0].shape, jnp.int32)
    for j in range(1, len(rows)):
        upd = rows[j] > best
        best = jnp.where(upd, rows[j], best)
        bi = jnp.where(upd, j, bi)
    sec, si = None, None
    for j in range(len(rows)):
        cand = jnp.where(bi == j, -jnp.inf, rows[j])
        if sec is None:
            sec, si = cand, jnp.zeros(cand.shape, jnp.int32)
        else:
            upd = cand > sec
            sec = jnp.where(upd, cand, sec)
            si = jnp.where(upd, j, si)
    return best, sec, bi, si


def _router_kernel(x_ref, wt_ref, b_ref, o_ref, *, n_groups):
    logits = lax.dot_general(wt_ref[...], x_ref[...], (((1,), (1,)), ((), ())),
                             preferred_element_type=F32,
                             precision=lax.Precision.HIGHEST) + b_ref[...]
    n_exp = logits.shape[0]
    per = n_exp // n_groups
    e = jnp.exp(logits - jnp.max(logits, axis=0, keepdims=True))
    probs = e / jnp.sum(e, axis=0, keepdims=True)
    picks = [_top2([probs[g * per + j:g * per + j + 1, :] for j in range(per)])
             for g in range(n_groups)]
    score = picks[0][0] + picks[0][1]
    sel = picks[0]
    g_sel = jnp.zeros(score.shape, jnp.int32)
    for g in range(1, n_groups):
        sc = picks[g][0] + picks[g][1]
        upd = sc > score
        score = jnp.where(upd, sc, score)
        g_sel = jnp.where(upd, g, g_sel)
        sel = tuple(jnp.where(upd, a, b) for a, b in zip(picks[g], sel))
    p1, p2, i1, i2 = sel
    denom = p1 + p2
    e1 = g_sel * per + i1
    e2 = g_sel * per + i2
    eidx = lax.broadcasted_iota(jnp.int32, logits.shape, 0)
    o_ref[...] = (jnp.where(eidx == e1, p1 / denom, 0.0)
                  + jnp.where(eidx == e2, p2 / denom, 0.0))


def _router(x, w_router, b_router):
    m, d = x.shape
    n_exp = w_router.shape[1]
    tm = _tile(m, 512)
    return pl.pallas_call(
        functools.partial(_router_kernel, n_groups=N_GROUPS),
        grid=(m // tm,),
        in_specs=[pl.BlockSpec((tm, d), lambda i: (i, 0)),
                  pl.BlockSpec((n_exp, d), lambda i: (0, 0)),
                  pl.BlockSpec((n_exp, 1), lambda i: (0, 0))],
        out_specs=pl.BlockSpec((n_exp, tm), lambda i: (0, i)),
        out_shape=jax.ShapeDtypeStruct((n_exp, m), F32),
        compiler_params=_params("parallel"),
        name="router",
    )(x, w_router.T, b_router.reshape(n_exp, 1))


def _moe_kernel(x_ref, gates_ref, wg_ref, wu_ref, wd_ref, g_ref, beta_ref, o_ref,
                xb_ref, acc_ref, *, alpha):
    e = pl.program_id(1)

    @pl.when(e == 0)
    def _():
        xb_ref[...] = x_ref[...].astype(BF16)
        acc_ref[...] = jnp.zeros(acc_ref.shape, F32)

    gates = gates_ref[...]
    lane = lax.broadcasted_iota(jnp.int32, gates.shape, 1)
    gate = jnp.sum(jnp.where(lane == e, gates, 0.0), axis=1, keepdims=True)
    xb = xb_ref[...]
    hg = _dot(xb, wg_ref[0])
    hu = _dot(xb, wu_ref[0])
    h = hg * jax.nn.sigmoid(hg) * hu * gate
    acc_ref[...] += _dot(h.astype(BF16), wd_ref[0])

    @pl.when(e == pl.num_programs(1) - 1)
    def _():
        o_ref[...] = _layer_norm(alpha * x_ref[...] + acc_ref[...], g_ref[...], beta_ref[...])


def _moe(x, gates, wg_bf, wu_bf, wd_bf, g, beta, alpha):
    m, d = x.shape
    n_exp, _, ff = wg_bf.shape
    tm = _tile(m, 512)
    vec = lambda v: v.reshape(1, d)
    row = pl.BlockSpec((1, d), lambda i, e: (0, 0))
    return pl.pallas_call(
        functools.partial(_moe_kernel, alpha=alpha),
        grid=(m // tm, n_exp),
        in_specs=[pl.BlockSpec((tm, d), lambda i, e: (i, 0)),
                  pl.BlockSpec((tm, n_exp), lambda i, e: (i, 0)),
                  pl.BlockSpec((1, d, ff), lambda i, e: (e, 0, 0)),
                  pl.BlockSpec((1, d, ff), lambda i, e: (e, 0, 0)),
                  pl.BlockSpec((1, ff, d), lambda i, e: (e, 0, 0)),
                  row, row],
        out_specs=pl.BlockSpec((tm, d), lambda i, e: (i, 0)),
        out_shape=jax.ShapeDtypeStruct((m, d), F32),
        scratch_shapes=[pltpu.VMEM((tm, d), BF16), pltpu.VMEM((tm, d), F32)],
        compiler_params=_params("parallel", "arbitrary"),
        name="moe",
    )(x, gates, wg_bf, wu_bf, wd_bf, vec(g), vec(beta))


def _moe_grouped_kernel(cnt_ref, x_ref, gt_ref, gext_ref, wg_ref, wu_ref, wd_ref, g_ref, beta_ref,
                        o_ref, xb_ref, xs_ref, gs_ref, ys_ref, rrow_ref, rcol_ref, before_ref,
                        *, alpha, per, tile):
    c = pl.program_id(0)
    e = pl.program_id(1)
    g = e // per
    j = e % per
    chunk, d = x_ref.shape
    n_exp = gt_ref.shape[0]
    n_tiles = (cnt_ref[c, g] + tile - 1) // tile

    @pl.when((c == 0) & (e == 0))
    def _():
        r = lax.broadcasted_iota(jnp.int32, (chunk, chunk), 0)
        cc = lax.broadcasted_iota(jnp.int32, (chunk, chunk), 1)
        before_ref[0] = jnp.where(r < cc, 1.0, 0.0).astype(BF16)
        before_ref[1] = jnp.where(cc < r, 1.0, 0.0).astype(BF16)

    @pl.when(e == 0)
    def _():
        xb_ref[...] = x_ref[...].astype(BF16)
        o_ref[...] = jnp.zeros(o_ref.shape, F32)
        gt = gt_ref[...]
        row8 = lax.broadcasted_iota(jnp.int32, (8, chunk), 0)
        member = jnp.zeros((8, chunk), F32)
        for grp in range(n_exp // per):
            in_grp = jnp.sum(gt[grp * per:(grp + 1) * per, :], axis=0, keepdims=True) > 0.0
            member = jnp.where((row8 == grp) & in_grp, 1.0, member)
        rank = _dot(member.astype(BF16), before_ref[0])
        rrow_ref[...] = jnp.where(member > 0.0, rank, -1.0)
        lane_l = lax.broadcasted_iota(jnp.int32, (LANES, LANES), 0)
        lane_g = lax.broadcasted_iota(jnp.int32, (LANES, LANES), 1)
        fold = jnp.where((lane_l < n_exp) & (lane_l // per == lane_g), 1.0, 0.0).astype(BF16)
        picked = jnp.where(gext_ref[...].astype(F32) > 0.0, 1.0, 0.0).astype(BF16)
        member_c = jnp.where(_dot(picked, fold) > 0.0, 1.0, 0.0)
        rank_c = _dot(before_ref[1], member_c.astype(BF16))
        rcol_ref[...] = jnp.where(member_c > 0.0, rank_c, -1.0)

    @pl.when(j == 0)
    def _():
        rank_row = rrow_ref[pl.ds(g, 1), :]

        def gather(t, carry):
            rows = pl.ds(pl.multiple_of(t * tile, 16), tile)
            slot = t * tile + lax.broadcasted_iota(jnp.int32, (tile, 1), 0)
            onehot = jnp.where(rank_row == slot.astype(F32), 1.0, 0.0).astype(BF16)
            xs_ref[rows, :] = _dot(onehot, xb_ref[...]).astype(BF16)
            gs_ref[rows, :] = _dot(onehot, gext_ref[...])
            ys_ref[rows, :] = jnp.zeros((tile, d), F32)
            return carry

        lax.fori_loop(0, n_tiles, gather, 0)

    lane = lax.broadcasted_iota(jnp.int32, (tile, LANES), 1)
    gate_lanes = (lane == e) | (lane == e + n_exp)

    def expert(t, carry):
        rows = pl.ds(pl.multiple_of(t * tile, 16), tile)
        xs = xs_ref[rows, :]
        gate = jnp.sum(jnp.where(gate_lanes, gs_ref[rows, :], 0.0), axis=1, keepdims=True)
        hg = _dot(xs, wg_ref[0])
        hu = _dot(xs, wu_ref[0])
        h = hg * jax.nn.sigmoid(hg) * hu * gate
        ys_ref[rows, :] += _dot(h.astype(BF16), wd_ref[0])
        return carry

    lax.fori_loop(0, n_tiles, expert, 0)

    @pl.when(j == per - 1)
    def _():
        lane_c = lax.broadcasted_iota(jnp.int32, (chunk, LANES), 1)
        rank_col = jnp.sum(jnp.where(lane_c == g, rcol_ref[...], 0.0), axis=1, keepdims=True)

        def scatter(t, carry):
            rows = pl.ds(pl.multiple_of(t * tile, 16), tile)
            slot = t * tile + lax.broadcasted_iota(jnp.int32, (1, tile), 1)
            onehot_t = jnp.where(rank_col == slot.astype(F32), 1.0, 0.0).astype(BF16)
            o_ref[...] += _dot(onehot_t, ys_ref[rows, :].astype(BF16))
            return carry

        lax.fori_loop(0, n_tiles, scatter, 0)

    @pl.when(e == pl.num_programs(1) - 1)
    def _():
        o_ref[...] = _layer_norm(alpha * x_ref[...] + o_ref[...], g_ref[...], beta_ref[...])


def _moe_grouped(x, gates_t, wg_bf, wu_bf, wd_bf, g, beta, alpha):
    m, d = x.shape
    n_exp, _, ff = wg_bf.shape
    per = n_exp // N_GROUPS
    chunk = _tile(m, MOE_CHUNK)
    tile = min(chunk, -(-(chunk // N_GROUPS * 9 // 8) // 16) * 16)
    slots = -(-chunk // tile) * tile
    assert tile % 16 == 0 and 2 * n_exp <= LANES and N_GROUPS <= 8
    gates = gates_t.T
    hi = gates.astype(BF16)
    lo = (gates - hi.astype(F32)).astype(BF16)
    gext = jnp.concatenate([hi, lo, jnp.zeros((m, LANES - 2 * n_exp), BF16)], axis=1)
    counts = jnp.sum(jnp.sum(gates.reshape(m // chunk, chunk, N_GROUPS, per), axis=3) > 0,
                     axis=1).astype(jnp.int32)
    vec = lambda v: v.reshape(1, d)
    row = pl.BlockSpec((1, d), lambda i, e, cnt: (0, 0))
    return pl.pallas_call(
        functools.partial(_moe_grouped_kernel, alpha=alpha, per=per, tile=tile),
        grid_spec=pltpu.PrefetchScalarGridSpec(
            num_scalar_prefetch=1,
            grid=(m // chunk, n_exp),
            in_specs=[pl.BlockSpec((chunk, d), lambda i, e, cnt: (i, 0)),
                      pl.BlockSpec((n_exp, chunk), lambda i, e, cnt: (0, i)),
                      pl.BlockSpec((chunk, LANES), lambda i, e, cnt: (i, 0)),
                      pl.BlockSpec((1, d, ff), lambda i, e, cnt: (e, 0, 0)),
                      pl.BlockSpec((1, d, ff), lambda i, e, cnt: (e, 0, 0)),
                      pl.BlockSpec((1, ff, d), lambda i, e, cnt: (e, 0, 0)),
                      row, row],
            out_specs=pl.BlockSpec((chunk, d), lambda i, e, cnt: (i, 0)),
            scratch_shapes=[pltpu.VMEM((chunk, d), BF16),
                            pltpu.VMEM((slots, d), BF16),
                            pltpu.VMEM((slots, LANES), F32),
                            pltpu.VMEM((slots, d), F32),
                            pltpu.VMEM((8, chunk), F32),
                            pltpu.VMEM((chunk, LANES), F32),
                            pltpu.VMEM((2, chunk, chunk), BF16)]),
        out_shape=jax.ShapeDtypeStruct((m, d), F32),
        compiler_params=_params("arbitrary", "arbitrary"),
        name="moe_grouped",
    )(counts, x, gates_t, gext, wg_bf, wu_bf, wd_bf, vec(g), vec(beta))


def _sb_attend(zs, values, u_neg, mask, overlapped=None):
    n = len(zs)
    first_half, out = [], []
    for t in range(n + 1):
        if t < n:
            z = zs[t]() if callable(zs[t]) else zs[t]
            neg_abs = pltpu.bitcast(pltpu.bitcast(z, jnp.uint32) | jnp.uint32(0x80000000), F32)
            soft = jnp.log(1.0 + jnp.exp2(neg_abs)) * LOG2E
            lb = jnp.minimum(z, 0.0) - soft
            nlk = z - lb
            if mask is not None:
                nlk = jnp.where(mask, nlk, 0.0)
            nlk = nlk.astype(BF16)
            first_half.append((nlk, lb, _dot(nlk, u_neg)))
        if t >= 1:
            nlk, lb, later = first_half[t - 1]
            w = jnp.exp2(lb + later)
            if mask is not None:
                w = jnp.where(mask, w, 0.0)
            out.append((values[t - 1](w.astype(BF16)), later[:, 0:1] - nlk[:, 0:1].astype(F32)))
            if overlapped is not None:
                overlapped[t - 1]()
    return out


def _u_neg(size):
    r = lax.broadcasted_iota(jnp.int32, (size, size), 0)
    c = lax.broadcasted_iota(jnp.int32, (size, size), 1)
    return jnp.where(r > c, -1.0, 0.0).astype(BF16)


def _sb_prompt_kernel(bias_ref, q_ref, kt_ref, vt_ref, o_ref, acc_ref, s_ref, z_ref,
                      *, blk, head_dim):
    hp = pl.program_id(0)
    i = pl.program_id(1)
    heads_per = q_ref.shape[1] // head_dim
    q2 = q_ref[...]
    head_of_lane = lax.broadcasted_iota(jnp.int32, (1, q2.shape[1]), 1) // head_dim
    u_neg = _u_neg(blk)
    qs = [jnp.where(head_of_lane == h, q2, jnp.zeros_like(q2)) for h in range(heads_per)]
    biases = [bias_ref[hp * heads_per + h] for h in range(heads_per)]

    def units_of(key_blocks):
        offs = [pl.multiple_of(kb * blk, blk) for kb in key_blocks]
        return [(o, h) for o in offs for h in range(heads_per)]

    def logit_thunks(key_blocks):
        return [functools.partial(lambda o, h: _dot(qs[h], kt_ref[:, pl.ds(o, blk)]) + biases[h],
                                  o, h) for o, h in units_of(key_blocks)]

    def visit(zs, key_blocks, mask, first, overlapped=None):
        units = units_of(key_blocks)
        values = [functools.partial(lambda o, w: _dot_nt(w, vt_ref[:, pl.ds(o, blk)]), o)
                  for o, _ in units]
        results = _sb_attend(zs, values, u_neg, mask, overlapped)
        for (pv, total), (o, h) in zip(results, units):
            if first:
                acc_ref[h] = pv
                s_ref[h] = total
            else:
                acc_ref[h] += jnp.exp2(s_ref[h]) * pv
                s_ref[h] += total

    r = lax.broadcasted_iota(jnp.int32, (blk, blk), 0)
    c = lax.broadcasted_iota(jnp.int32, (blk, blk), 1)
    visit(logit_thunks([i]), [i], c < r, True)

    done = 0
    run = 1
    while run < SB_KEY_BLOCKS_PER_STEP:
        top = i - done

        @pl.when(i & run != 0)
        def _(top=top, run=run):
            blocks = [top - 1 - j for j in range(run)]
            visit(logit_thunks(blocks), blocks, None, False)

        done = done + (i & run)
        run *= 2

    n_steps = i // SB_KEY_BLOCKS_PER_STEP
    n_units = SB_KEY_BLOCKS_PER_STEP * heads_per

    def step_blocks(step):
        top = i - i % SB_KEY_BLOCKS_PER_STEP - SB_KEY_BLOCKS_PER_STEP * step
        return [jnp.maximum(top - 1 - j, 0) for j in range(SB_KEY_BLOCKS_PER_STEP)]

    def store_logits(u, thunk):
        z_ref[u] = thunk()

    @pl.when(n_steps > 0)
    def _():
        for u, thunk in enumerate(logit_thunks(step_blocks(0))):
            store_logits(u, thunk)

    def body(step, carry):
        refill = [functools.partial(store_logits, u, thunk)
                  for u, thunk in enumerate(logit_thunks(step_blocks(step + 1)))]
        current = [functools.partial(lambda u: z_ref[u], u) for u in range(n_units)]
        visit(current, step_blocks(step), None, False, refill)
        return carry

    lax.fori_loop(0, n_steps, body, 0)
    out = acc_ref[0]
    for h in range(1, heads_per):
        out = jnp.where(head_of_lane == h, acc_ref[h], out)
    o_ref[...] = out


def _sb_prompt(q_bf, kt_bf, vt_bf, bias2, head_dim):
    n, width = q_bf.shape
    n_groups = width // LANES
    blk = _tile(n, 256)
    return pl.pallas_call(
        functools.partial(_sb_prompt_kernel, blk=blk, head_dim=head_dim),
        grid=(n_groups, n // blk),
        in_specs=[pl.BlockSpec(memory_space=pltpu.SMEM),
                  pl.BlockSpec((blk, LANES), lambda p, i: (i, p)),
                  pl.BlockSpec((LANES, n), lambda p, i: (p, 0)),
                  pl.BlockSpec((LANES, n), lambda p, i: (p, 0))],
        out_specs=pl.BlockSpec((blk, LANES), lambda p, i: (i, p)),
        out_shape=jax.ShapeDtypeStruct((n, width), F32),
        scratch_shapes=[pltpu.VMEM((LANES // head_dim, blk, LANES), F32),
                        pltpu.VMEM((LANES // head_dim, blk, 1), F32),
                        pltpu.VMEM((SB_KEY_BLOCKS_PER_STEP * (LANES // head_dim), blk, blk), F32)],
        compiler_params=_params("parallel", "arbitrary"),
        name="sb_prompt",
    )(bias2, q_bf, kt_bf, vt_bf)


def _sb_sample_kernel(pt_ref, qbd_ref, knew_ref, vnew_ref, bias_ref, kpool_ref, vpool_ref,
                      o_ref, kbuf, vbuf, sem, pad_k, pad_v, acc_ref,
                      *, n_pages, page, group, sub, t_new):
    b = pl.program_id(0)
    n_steps = n_pages // group

    def copies(step, slot):
        out = []
        for r in range(group):
            phys = pt_ref[b, (n_steps - 1 - step) * group + r]
            out.append(pltpu.make_async_copy(kpool_ref.at[phys], kbuf.at[slot, r], sem.at[0, slot]))
            out.append(pltpu.make_async_copy(vpool_ref.at[phys], vbuf.at[slot, r], sem.at[1, slot]))
        return out

    for ahead in range(min(SB_PAGE_SLOTS - 1, n_steps)):
        for cp in copies(ahead, ahead):
            cp.start()

    qbd = qbd_ref[0]
    bias = bias_ref[...]
    rows = qbd.shape[0]

    pad_k[...] = jnp.zeros(pad_k.shape, F32)
    pad_v[...] = jnp.zeros(pad_v.shape, F32)
    pad_k[0:t_new, :] = knew_ref[0]
    pad_v[0:t_new, :] = vnew_ref[0]
    t_of_row = lax.broadcasted_iota(jnp.int32, (rows, page), 0) % SB_ROWS_PER_HEAD
    s_of_col = lax.broadcasted_iota(jnp.int32, (rows, page), 1)
    (pv, s_after), = _sb_attend([_dot_nt(qbd, pad_k[...].astype(BF16)) + bias],
                                [lambda w: _dot(w, pad_v[...].astype(BF16))], _u_neg(page),
                                s_of_col < t_of_row)
    acc_ref[...] = pv

    per_sub = sub // page
    u_neg = _u_neg(sub)

    def body(step, s_after):
        slot = step % SB_PAGE_SLOTS
        for cp in copies(step, slot):
            cp.wait()

        @pl.when(step + SB_PAGE_SLOTS - 1 < n_steps)
        def _():
            for cp in copies(step + SB_PAGE_SLOTS - 1, (step + SB_PAGE_SLOTS - 1) % SB_PAGE_SLOTS):
                cp.start()

        def keys_of(buf, first):
            return jnp.concatenate([buf[slot, first + j].astype(BF16) for j in range(per_sub)],
                                   axis=1)

        firsts = list(reversed(range(0, group, per_sub)))
        zs = [_dot(qbd, keys_of(kbuf, f)) + bias for f in firsts]
        values = [functools.partial(lambda f, w: _dot_nt(w, keys_of(vbuf, f)), f) for f in firsts]
        upd = jnp.zeros(acc_ref.shape, F32)
        for pv, total in _sb_attend(zs, values, u_neg, None):
            upd = upd + jnp.exp2(s_after) * pv
            s_after = s_after + total
        acc_ref[...] += upd
        return s_after

    lax.fori_loop(0, n_steps, body, s_after)

    width = acc_ref.shape[1]
    n_heads = rows // SB_ROWS_PER_HEAD
    head_dim = width // n_heads
    head_of_row = lax.broadcasted_iota(jnp.int32, (rows, width), 0) // SB_ROWS_PER_HEAD
    head_of_col = lax.broadcasted_iota(jnp.int32, (rows, width), 1) // head_dim
    own = jnp.where(head_of_row == head_of_col, acc_ref[...], 0.0)
    out = own[0:SB_ROWS_PER_HEAD, :]
    for h in range(1, n_heads):
        out = out + own[h * SB_ROWS_PER_HEAD:(h + 1) * SB_ROWS_PER_HEAD, :]
    o_ref[0] = out[0:t_new, :]


def _sb_sample(q, k_new, v_new, bias2, pool_kt, pool_vt, page_table, q_scale):
    b, t_new, width = q.shape
    n_heads = bias2.shape[0]
    head_dim = width // n_heads
    page = pool_kt.shape[2]
    n_pages = page_table.shape[1]
    group = max(g for g in (4, 2, 1) if n_pages % g == 0)
    sub = min(2, group) * page
    rows = n_heads * SB_ROWS_PER_HEAD
    head_of_col = jnp.arange(width) // head_dim
    qpad = jnp.pad(q * q_scale, ((0, 0), (0, SB_ROWS_PER_HEAD - t_new), (0, 0)))
    qbd = jnp.where(head_of_col[None, None, None, :] == jnp.arange(n_heads)[None, :, None, None],
                    qpad[:, None, :, :], 0.0).reshape(b, rows, width).astype(BF16)
    bias_rows = jnp.repeat(bias2, SB_ROWS_PER_HEAD).reshape(rows, 1)
    return pl.pallas_call(
        functools.partial(_sb_sample_kernel, n_pages=n_pages, page=page, group=group, sub=sub,
                          t_new=t_new),
        grid_spec=pltpu.PrefetchScalarGridSpec(
            num_scalar_prefetch=1,
            grid=(b,),
            in_specs=[pl.BlockSpec((1, rows, width), lambda i, pt: (i, 0, 0)),
                      pl.BlockSpec((1, t_new, width), lambda i, pt: (i, 0, 0)),
                      pl.BlockSpec((1, t_new, width), lambda i, pt: (i, 0, 0)),
                      pl.BlockSpec((rows, 1), lambda i, pt: (0, 0)),
                      pl.BlockSpec(memory_space=pl.ANY),
                      pl.BlockSpec(memory_space=pl.ANY)],
            out_specs=pl.BlockSpec((1, t_new, width), lambda i, pt: (i, 0, 0)),
            scratch_shapes=[pltpu.VMEM((SB_PAGE_SLOTS, group, width, page), F32),
                            pltpu.VMEM((SB_PAGE_SLOTS, group, width, page), F32),
                            pltpu.SemaphoreType.DMA((2, SB_PAGE_SLOTS)),
                            pltpu.VMEM((page, width), F32),
                            pltpu.VMEM((page, width), F32),
                            pltpu.VMEM((rows, width), F32)]),
        out_shape=jax.ShapeDtypeStruct((b, t_new, width), F32),
        compiler_params=_params("arbitrary"),
        name="sb_sample",
    )(page_table, qbd, k_new, v_new, bias_rows, pool_kt, pool_vt)


def _feature_major(t):
    lead = t.shape[:-3]
    tokens, heads, dim = t.shape[-3:]
    n = len(lead)
    return jnp.transpose(t, (*range(n), n + 1, n + 2, n)).reshape(*lead, heads * dim, tokens)


def _token_major(t, heads):
    lead = t.shape[:-2]
    width, tokens = t.shape[-2:]
    n = len(lead)
    t = t.reshape(*lead, heads, width // heads, tokens)
    return jnp.transpose(t, (*range(n), n + 2, n, n + 1))


def kernel(x_prompt, x_sample, state_conv, cache_sb_k, cache_sb_v, cache_mem_k, cache_mem_v, page_table, mem_prompt, a_w_in, a_b_in, a_conv_w, a_conv_b, a_norm_g, a_norm_b, a_w_out, a_b_out, b_w_in, b_b_in, b_sb_bias, b_w_out, b_b_out, w_mem_kv, ln_mix_g, ln_mix_b, ln_ffn_g, ln_ffn_b, w_router, b_router, w_exp_gate, w_exp_up, w_exp_down):
    depth = w_mem_kv.shape[0]
    n_mixers = 2
    alpha = (2.0 * depth) ** 0.25
    bp, seq, d = x_prompt.shape
    bs, t_new, _ = x_sample.shape
    mem_heads, mem_hd = cache_mem_k.shape[-2:]
    mem_w = mem_heads * mem_hd
    sb_heads, sb_hd = cache_sb_k.shape[-2:]
    sb_w = sb_heads * sb_hd
    conv_ch = a_conv_w.shape[-1]
    mem_tokens = mem_prompt.shape[1]
    assert bp == 1
    sb_q_scale = float(sb_hd) ** -0.5 * LOG2E

    y_p = x_prompt.reshape(seq, d)
    y_s = x_sample.reshape(bs * t_new, d)
    mem_p = mem_prompt.reshape(mem_tokens, d)
    conv_p, conv_s, sbk_p, sbv_p, sbk_s, sbv_s, memk_p, memv_p = ([] for _ in range(8))

    for layer in range(depth):
        kvt = _linear_t(mem_p, w_mem_kv[layer].T.astype(BF16), name="mem_kv")
        mkt_p = kvt[:mem_w][None]
        mvt_p = kvt[mem_w:][None]
        memk_p.append(_token_major(mkt_p, mem_heads))
        memv_p.append(_token_major(mvt_p, mem_heads))
        mkt_s = _feature_major(cache_mem_k[layer])
        mvt_s = _feature_major(cache_mem_v[layer])
        i = layer // n_mixers
        if layer % n_mixers == 0:
            w_in = a_w_in[i].astype(BF16)
            proj_p = _linear(y_p, w_in, a_b_in[i], name="conv_in_proj")
            proj_s = _linear(y_s, w_in, a_b_in[i], name="conv_in_proj_s")
            mix_p, st_p = _conv_prompt(proj_p, a_conv_w[i], a_conv_b[i], a_norm_g[i], a_norm_b[i])
            conv_p.append(st_p[None])
            proj_tm = jnp.transpose(proj_s.reshape(bs, t_new, -1), (1, 0, 2))
            mix_s, st_s = _conv_sample(proj_tm, jnp.transpose(state_conv[i], (1, 0, 2)),
                                       a_conv_w[i], a_conv_b[i], a_norm_g[i], a_norm_b[i])
            mix_s = jnp.transpose(mix_s, (1, 0, 2)).reshape(bs * t_new, conv_ch)
            conv_s.append(jnp.transpose(st_s, (1, 0, 2)))
            q_col = (2 * conv_ch) // mem_w
            assert q_col * mem_w == 2 * conv_ch
            qm_p, qm_s = proj_p[None], proj_s.reshape(bs, t_new, -1)
            w_out, b_out = a_w_out[i], a_b_out[i]
        else:
            bias2 = b_sb_bias[i] * LOG2E
            q_bf, kt, vt, kt_bf, vt_bf, qm = _sb_in_proj(y_p, b_w_in[i], b_b_in[i], sb_w, sb_q_scale)
            sbk_p.append(_token_major(kt[None], sb_heads))
            sbv_p.append(_token_major(vt[None], sb_heads))
            mix_p = _sb_prompt(q_bf, kt_bf, vt_bf, bias2, sb_hd)
            proj_s = _linear(y_s, b_w_in[i].astype(BF16), b_b_in[i], name="sb_in_proj_s")
            proj_s3 = proj_s.reshape(bs, t_new, -1)
            k_s = proj_s3[:, :, sb_w:2 * sb_w]
            v_s = proj_s3[:, :, 2 * sb_w:3 * sb_w]
            sbk_s.append(k_s.reshape(bs, t_new, sb_heads, sb_hd))
            sbv_s.append(v_s.reshape(bs, t_new, sb_heads, sb_hd))
            mix_s = _sb_sample(proj_s3[:, :, :sb_w], k_s, v_s, bias2,
                               _feature_major(cache_sb_k[i]), _feature_major(cache_sb_v[i]),
                               page_table, sb_q_scale)
            mix_s = mix_s.reshape(bs * t_new, sb_w)
            q_col = (3 * sb_w) // mem_w
            assert q_col * mem_w == 3 * sb_w
            qm_p, qm_s = qm[None], proj_s3
            q_col_p = 0
            w_out, b_out = b_w_out[i], b_b_out[i]

        mem_out_p = _mem_attn(qm_p, q_col if layer % n_mixers == 0 else q_col_p, mkt_p, mvt_p,
                              mem_heads)
        mem_out_s = _mem_attn(qm_s, q_col, mkt_s, mvt_s, mem_heads)
        w_out_bf = w_out.astype(BF16)
        y_p = _out_proj(y_p, mix_p, mem_out_p.reshape(seq, mem_w), w_out_bf, b_out,
                        ln_mix_g[layer], ln_mix_b[layer], alpha)
        y_s = _out_proj(y_s, mix_s, mem_out_s.reshape(bs * t_new, mem_w), w_out_bf, b_out,
                        ln_mix_g[layer], ln_mix_b[layer], alpha)

        wg, wu, wd = (w[layer].astype(BF16) for w in (w_exp_gate, w_exp_up, w_exp_down))
        y_p = _moe_grouped(y_p, _router(y_p, w_router, b_router), wg, wu, wd,
                           ln_ffn_g[layer], ln_ffn_b[layer], alpha)
        y_s = _moe(y_s, _router(y_s, w_router, b_router).T, wg, wu, wd,
                   ln_ffn_g[layer], ln_ffn_b[layer], alpha)

    return (y_p.reshape(bp, seq, d), y_s.reshape(bs, t_new, d),
            jnp.stack(conv_p), jnp.stack(conv_s), jnp.stack(sbk_p), jnp.stack(sbv_p),
            jnp.stack(sbk_s), jnp.stack(sbv_s), jnp.stack(memk_p), jnp.stack(memv_p))
```

```python
import functools
import math

import jax
import jax.numpy as jnp
from jax import lax
from jax.experimental import pallas as pl
from jax.experimental.pallas import tpu as pltpu

LN_EPS = 1e-5
N_GROUPS = 4
TOP_K = 2
CONV_HALO = 32
LANES = 128
MATMUL_ROWS = 512
SB_ROWS_PER_HEAD = 8
SB_KEY_BLOCKS_PER_STEP = 8
SB_PAGE_SLOTS = 3
SB_BIAS_ROWS = 16
SB_BIAS_PARTS = 3
MOE_CHUNK = 1024
VMEM_LIMIT = 48 * 1024 * 1024
LOG2E = math.log2(math.e)

BF16 = jnp.bfloat16
F32 = jnp.float32


def _tile(n, pref):
    if n <= pref:
        return n
    t = pref - pref % 8
    while t >= 8:
        if n % t == 0:
            return t
        t -= 8
    return n


def _params(*sem):
    return pltpu.CompilerParams(dimension_semantics=sem, vmem_limit_bytes=VMEM_LIMIT)


def _layer_norm(x, g, b):
    mu = jnp.mean(x, axis=-1, keepdims=True)
    xc = x - mu
    var = jnp.mean(xc * xc, axis=-1, keepdims=True)
    return xc * lax.rsqrt(var + LN_EPS) * g + b


def _dot(a, b):
    return jnp.dot(a, b, preferred_element_type=F32)


def _dot_nt(a, b):
    return lax.dot_general(a, b, (((1,), (1,)), ((), ())), preferred_element_type=F32)


def _linear_kernel(x_ref, w_ref, b_ref, o_ref):
    o_ref[...] = _dot(x_ref[...].astype(BF16), w_ref[...]) + b_ref[...]


def _linear(x, w_bf, b, *, name="linear"):
    m, k = x.shape
    n = w_bf.shape[1]
    tm = _tile(m, MATMUL_ROWS)
    return pl.pallas_call(
        _linear_kernel,
        grid=(m // tm,),
        in_specs=[pl.BlockSpec((tm, k), lambda i: (i, 0)),
                  pl.BlockSpec((k, n), lambda i: (0, 0)),
                  pl.BlockSpec((1, n), lambda i: (0, 0))],
        out_specs=pl.BlockSpec((tm, n), lambda i: (i, 0)),
        out_shape=jax.ShapeDtypeStruct((m, n), F32),
        compiler_params=_params("parallel"),
        name=name,
    )(x, w_bf, b.reshape(1, n))


def _linear_t_kernel(x_ref, wt_ref, o_ref):
    o_ref[...] = _dot_nt(wt_ref[...], x_ref[...].astype(BF16))


def _linear_t(x, wt_bf, *, name="linear_t"):
    m, k = x.shape
    n = wt_bf.shape[0]
    tm = _tile(m, 256)
    return pl.pallas_call(
        _linear_t_kernel,
        grid=(m // tm,),
        in_specs=[pl.BlockSpec((tm, k), lambda i: (i, 0)),
                  pl.BlockSpec((n, k), lambda i: (0, 0))],
        out_specs=pl.BlockSpec((n, tm), lambda i: (0, i)),
        out_shape=jax.ShapeDtypeStruct((n, m), F32),
        compiler_params=_params("parallel"),
        name=name,
    )(x, wt_bf)


def _sb_in_proj_kernel(x_ref, wq_ref, wkt_ref, wvt_ref, wm_ref, bq_ref, bk_ref, bv_ref, bm_ref,
                       q_ref, kt_ref, vt_ref, ktb_ref, vtb_ref, qm_ref, *, q_scale):
    xb = x_ref[...].astype(BF16)
    q_ref[...] = ((_dot(xb, wq_ref[...]) + bq_ref[...]) * q_scale).astype(BF16)
    kt = _dot_nt(wkt_ref[...], xb) + bk_ref[...]
    kt_ref[...] = kt
    ones = jnp.ones((SB_BIAS_ROWS, kt.shape[1]), BF16)
    for p in range(kt.shape[0] // LANES):
        base = p * (LANES + SB_BIAS_ROWS)
        ktb_ref[base:base + LANES, :] = kt[p * LANES:(p + 1) * LANES].astype(BF16)
        ktb_ref[base + LANES:base + LANES + SB_BIAS_ROWS, :] = ones
    vt = _dot_nt(wvt_ref[...], xb) + bv_ref[...]
    vt_ref[...] = vt
    vtb_ref[...] = vt.astype(BF16)
    qm_ref[...] = _dot(xb, wm_ref[...]) + bm_ref[...]


def _sb_in_proj(x, w_in, b_in, sb_w, q_scale):
    n, d = x.shape
    wm = w_in.shape[1] - 3 * sb_w
    tm = _tile(n, MATMUL_ROWS)
    kt_rows =sb_w // LANES * (LANES + SB_BIAS_ROWS)
    wq = w_in[:, :sb_w].astype(BF16)
    wkt = w_in[:, sb_w:2 * sb_w].T.astype(BF16)
    wvt = w_in[:, 2 * sb_w:3 * sb_w].T.astype(BF16)
    wmem = w_in[:, 3 * sb_w:].astype(BF16)
    const = lambda shape: pl.BlockSpec(shape, lambda i: (0, 0))
    return pl.pallas_call(
        functools.partial(_sb_in_proj_kernel, q_scale=q_scale),
        grid=(n // tm,),
        in_specs=[pl.BlockSpec((tm, d), lambda i: (i, 0)),
                  const((d, sb_w)), const((sb_w, d)), const((sb_w, d)), const((d, wm)),
                  const((1, sb_w)), const((sb_w, 1)), const((sb_w, 1)), const((1, wm))],
        out_specs=[pl.BlockSpec((tm, sb_w), lambda i: (i, 0)),
                   pl.BlockSpec((sb_w, tm), lambda i: (0, i)),
                   pl.BlockSpec((sb_w, tm), lambda i: (0, i)),
                   pl.BlockSpec((kt_rows, tm), lambda i: (0, i)),
                   pl.BlockSpec((sb_w, tm), lambda i: (0, i)),
                   pl.BlockSpec((tm, wm), lambda i: (i, 0))],
        out_shape=[jax.ShapeDtypeStruct((n, sb_w), BF16),
                   jax.ShapeDtypeStruct((sb_w, n), F32),
                   jax.ShapeDtypeStruct((sb_w, n), F32),
                   jax.ShapeDtypeStruct((kt_rows, n), BF16),
                   jax.ShapeDtypeStruct((sb_w, n), BF16),
                   jax.ShapeDtypeStruct((n, wm), F32)],
        compiler_params=_params("parallel"),
        name="sb_in_proj",
    )(x, wq, wkt, wvt, wmem,
      b_in[:sb_w].reshape(1, sb_w), b_in[sb_w:2 * sb_w].reshape(sb_w, 1),
      b_in[2 * sb_w:3 * sb_w].reshape(sb_w, 1), b_in[3 * sb_w:].reshape(1, wm))


def _conv_prompt_kernel(a_ref, gate_ref, w_ref, cb_ref, ng_ref, nb_ref,
                        mix_ref, st_ref, f_ref, g_ref, *, tile, width, chunk):
    i = pl.program_id(0)
    halo = CONV_HALO
    sub = 8

    @pl.when(i == 0)
    def _():
        f_ref[0:halo, :] = jnp.zeros((halo, f_ref.shape[1]), F32)

    f_ref[halo:halo + tile, :] = a_ref[...] * jax.nn.sigmoid(gate_ref[...])
    shifted_rows = g_ref.shape[1]
    for b in range(1, sub):
        g_ref[b - 1] = f_ref[b:b + shifted_rows, :]
    first = halo - (width - 1)
    for c in range(tile // chunk):
        y = jnp.zeros((chunk, f_ref.shape[1]), F32) + cb_ref[...]
        for k in range(width):
            b = (first + k) % sub
            row = c * chunk + (first + k) - b
            src = f_ref[row:row + chunk, :] if b == 0 else g_ref[b - 1, row:row + chunk, :]
            y = y + src * w_ref[k:k + 1, :]
        y = _layer_norm(y, ng_ref[...], nb_ref[...])
        mix_ref[c * chunk:(c + 1) * chunk, :] = y * jax.nn.sigmoid(y)

    @pl.when(i == pl.num_programs(0) - 1)
    def _():
        st_ref[...] = f_ref[halo + tile - (width - 1):halo + tile, :]

    f_ref[0:halo, :] = f_ref[tile:tile + halo, :]


def _conv_prompt(proj, conv_w, conv_b, norm_g, norm_b):
    n = proj.shape[0]
    width, c = conv_w.shape
    tile = _tile(n, 256)
    chunk = _tile(tile, 32)
    vec = lambda v: v.reshape(1, c)
    row = pl.BlockSpec((1, c), lambda i: (0, 0))
    return pl.pallas_call(
        functools.partial(_conv_prompt_kernel, tile=tile, width=width, chunk=chunk),
        grid=(n // tile,),
        in_specs=[pl.BlockSpec((tile, c), lambda i: (i, 0)),
                  pl.BlockSpec((tile, c), lambda i: (i, 1)),
                  pl.BlockSpec((width, c), lambda i: (0, 0)),
                  row, row, row],
        out_specs=[pl.BlockSpec((tile, c), lambda i: (i, 0)),
                   pl.BlockSpec((width - 1, c), lambda i: (0, 0))],
        out_shape=[jax.ShapeDtypeStruct((n, c), F32),
                   jax.ShapeDtypeStruct((width - 1, c), F32)],
        scratch_shapes=[pltpu.VMEM((tile + CONV_HALO, c), F32),
                        pltpu.VMEM((7, tile + CONV_HALO - 8, c), F32)],
        compiler_params=_params("arbitrary"),
        name="conv_prompt",
    )(proj, proj, conv_w, vec(conv_b), vec(norm_g), vec(norm_b))


def _conv_sample_kernel(a_ref, gate_ref, past_ref, w_ref, cb_ref, ng_ref, nb_ref,
                        mix_ref, st_ref, *, t_new, width):
    hist = width - 1
    new = [a_ref[t] * jax.nn.sigmoid(gate_ref[t]) for t in range(t_new)]
    full = lambda j: past_ref[j] if j < hist else new[j - hist]
    for t in range(t_new):
        y = jnp.zeros(new[0].shape, F32) + cb_ref[...]
        for k in range(width):
            y = y + full(t + k) * w_ref[k:k + 1, :]
        y = _layer_norm(y, ng_ref[...], nb_ref[...])
        mix_ref[t] = y * jax.nn.sigmoid(y)
    for j in range(hist):
        st_ref[j] = full(j + t_new)


def _conv_sample(proj_tm, past_tm, conv_w, conv_b, norm_g, norm_b):
    t_new, batch, _ = proj_tm.shape
    width, c = conv_w.shape
    vec = lambda v: v.reshape(1, c)
    row = pl.BlockSpec((1, c), lambda i: (0, 0))
    return pl.pallas_call(
        functools.partial(_conv_sample_kernel, t_new=t_new, width=width),
        grid=(1,),
        in_specs=[pl.BlockSpec((t_new, batch, c), lambda i: (0, 0, 0)),
                  pl.BlockSpec((t_new, batch, c), lambda i: (0, 0, 1)),
                  pl.BlockSpec((width - 1, batch, c), lambda i: (0, 0, 0)),
                  pl.BlockSpec((width, c), lambda i: (0, 0)),
                  row, row, row],
        out_specs=[pl.BlockSpec((t_new, batch, c), lambda i: (0, 0, 0)),
                   pl.BlockSpec((width - 1, batch, c), lambda i: (0, 0, 0))],
        out_shape=[jax.ShapeDtypeStruct((t_new, batch, c), F32),
                   jax.ShapeDtypeStruct((width - 1, batch, c), F32)],
        compiler_params=_params("arbitrary"),
        name="conv_sample",
    )(proj_tm, proj_tm, past_tm, conv_w, vec(conv_b), vec(norm_g), vec(norm_b))


def _mem_attn_kernel(q_ref, kt_ref, vt_ref, o_ref, *, heads, scale):
    q = q_ref[0]
    kt = kt_ref[0].astype(BF16)
    vt = vt_ref[0].astype(BF16)
    width = q.shape[-1]
    head_of_lane = lax.broadcasted_iota(jnp.int32, (1, width), 1) // (width // heads)
    out = jnp.zeros(q.shape, F32)
    for h in range(heads):
        sel = head_of_lane == h
        s = _dot(jnp.where(sel, q, 0.0).astype(BF16), kt) * scale
        p = jnp.exp(s - jnp.max(s, axis=-1, keepdims=True))
        p = p / jnp.sum(p, axis=-1, keepdims=True)
        out = out + jnp.where(sel, _dot_nt(p.astype(BF16), vt), 0.0)
    o_ref[0] = out


def _mem_attn(q_src, col_block, mem_kt, mem_vt, heads):
    b, t, _ = q_src.shape
    width, m = mem_kt.shape[1:]
    tq = _tile(t, 512)
    scale = float(width // heads) ** -0.5
    return pl.pallas_call(
        functools.partial(_mem_attn_kernel, heads=heads, scale=scale),
        grid=(b, t // tq),
        in_specs=[pl.BlockSpec((1, tq, width), lambda i, j: (i, j, col_block)),
                  pl.BlockSpec((1, width, m), lambda i, j: (i, 0, 0)),
                  pl.BlockSpec((1, width, m), lambda i, j: (i, 0, 0))],
        out_specs=pl.BlockSpec((1, tq, width), lambda i, j: (i, j, 0)),
        out_shape=jax.ShapeDtypeStruct((b, t, width), F32),
        compiler_params=_params("parallel", "parallel"),
        name="mem_attn",
    )(q_src, mem_kt, mem_vt)


def _out_proj_kernel(x_ref, mix_ref, mem_ref, wa_ref, wb_ref, b_ref, g_ref, beta_ref,
                     o_ref, *, alpha):
    acc = _dot(mix_ref[...].astype(BF16), wa_ref[...])
    acc = acc + _dot(mem_ref[...].astype(BF16), wb_ref[...]) + b_ref[...]
    o_ref[...] = _layer_norm(alpha * x_ref[...] + acc, g_ref[...], beta_ref[...])


def _out_proj(x, mix, mem, w_out_bf, b_out, g, beta, alpha):
    m, d = x.shape
    cm = mix.shape[1]
    cw = mem.shape[1]
    tm = _tile(m, MATMUL_ROWS)
    vec = lambda v: v.reshape(1, d)
    row = pl.BlockSpec((1, d), lambda i: (0, 0))
    return pl.pallas_call(
        functools.partial(_out_proj_kernel, alpha=alpha),
        grid=(m // tm,),
        in_specs=[pl.BlockSpec((tm, d), lambda i: (i, 0)),
                  pl.BlockSpec((tm, cm), lambda i: (i, 0)),
                  pl.BlockSpec((tm, cw), lambda i: (i, 0)),
                  pl.BlockSpec((cm, d), lambda i: (0, 0)),
                  pl.BlockSpec((cw, d), lambda i: (0, 0)),
                  row, row, row],
        out_specs=pl.BlockSpec((tm, d), lambda i: (i, 0)),
        out_shape=jax.ShapeDtypeStruct((m, d), F32),
        compiler_params=_params("parallel"),
        name="out_proj_ln",
    )(x, mix, mem, w_out_bf[:cm], w_out_bf[cm:], vec(b_out), vec(g), vec(beta))


def _top2(rows):
    best, bi = rows[0], jnp.zeros(rows[0].shape, jnp.int32)
    for j in range(1, len(rows)):
        upd = rows[j] > best
        best = jnp.where(upd, rows[j], best)
        bi = jnp.where(upd, j, bi)
    sec, si = None, None
    for j in range(len(rows)):
        cand = jnp.where(bi == j, -jnp.inf, rows[j])
        if sec is None:
            sec, si = cand, jnp.zeros(cand.shape, jnp.int32)
        else:
            upd = cand > sec
            sec = jnp.where(upd, cand, sec)
            si = jnp.where(upd, j, si)
    return best, sec, bi, si


def _router_kernel(x_ref, wt_ref, b_ref, o_ref, *, n_groups):
    logits = lax.dot_general(wt_ref[...], x_ref[...], (((1,), (1,)), ((), ())),
                             preferred_element_type=F32,
                             precision=lax.Precision.HIGHEST) + b_ref[...]
    n_exp = logits.shape[0]
    per = n_exp // n_groups
    e = jnp.exp(logits - jnp.max(logits, axis=0, keepdims=True))
    probs = e / jnp.sum(e, axis=0, keepdims=True)
    picks = [_top2([probs[g * per + j:g * per + j + 1, :] for j in range(per)])
             for g in range(n_groups)]
    score = picks[0][0] + picks[0][1]
    sel = picks[0]
    g_sel = jnp.zeros(score.shape, jnp.int32)
    for g in range(1, n_groups):
        sc = picks[g][0] + picks[g][1]
        upd = sc > score
        score = jnp.where(upd, sc, score)
        g_sel = jnp.where(upd, g, g_sel)
        sel = tuple(jnp.where(upd, a, b) for a, b in zip(picks[g], sel))
    p1, p2, i1, i2 = sel
    denom = p1 + p2
    e1 = g_sel * per + i1
    e2 = g_sel * per + i2
    eidx = lax.broadcasted_iota(jnp.int32, logits.shape, 0)
    o_ref[...] = (jnp.where(eidx == e1, p1 / denom, 0.0)
                  + jnp.where(eidx == e2, p2 / denom, 0.0))


def _router(x, w_router, b_router):
    m, d = x.shape
    n_exp = w_router.shape[1]
    tm = _tile(m, 512)
    return pl.pallas_call(
        functools.partial(_router_kernel, n_groups=N_GROUPS),
        grid=(m // tm,),
        in_specs=[pl.BlockSpec((tm, d), lambda i: (i, 0)),
                  pl.BlockSpec((n_exp, d), lambda i: (0, 0)),
                  pl.BlockSpec((n_exp, 1), lambda i: (0, 0))],
        out_specs=pl.BlockSpec((n_exp, tm), lambda i: (0, i)),
        out_shape=jax.ShapeDtypeStruct((n_exp, m), F32),
        compiler_params=_params("parallel"),
        name="router",
    )(x, w_router.T, b_router.reshape(n_exp, 1))


def _moe_kernel(x_ref, gates_ref, wg_ref, wu_ref, wd_ref, g_ref, beta_ref, o_ref,
                xb_ref, acc_ref, *, alpha):
    e = pl.program_id(1)

    @pl.when(e == 0)
    def _():
        xb_ref[...] = x_ref[...].astype(BF16)
        acc_ref[...] = jnp.zeros(acc_ref.shape, F32)

    gates = gates_ref[...]
    lane = lax.broadcasted_iota(jnp.int32, gates.shape, 1)
    gate = jnp.sum(jnp.where(lane == e, gates, 0.0), axis=1, keepdims=True)
    xb = xb_ref[...]
    hg = _dot(xb, wg_ref[0])
    hu = _dot(xb, wu_ref[0])
    h = hg * jax.nn.sigmoid(hg) * hu * gate
    acc_ref[...] += _dot(h.astype(BF16), wd_ref[0])

    @pl.when(e == pl.num_programs(1) - 1)
    def _():
        o_ref[...] = _layer_norm(alpha * x_ref[...] + acc_ref[...], g_ref[...], beta_ref[...])


def _moe(x, gates, wg_bf, wu_bf, wd_bf, g, beta, alpha):
    m, d = x.shape
    n_exp, _, ff = wg_bf.shape
    tm = _tile(m, 512)
    vec = lambda v: v.reshape(1, d)
    row = pl.BlockSpec((1, d), lambda i, e: (0, 0))
    return pl.pallas_call(
        functools.partial(_moe_kernel, alpha=alpha),
        grid=(m // tm, n_exp),
        in_specs=[pl.BlockSpec((tm, d), lambda i, e: (i, 0)),
                  pl.BlockSpec((tm, n_exp), lambda i, e: (i, 0)),
                  pl.BlockSpec((1, d, ff), lambda i, e: (e, 0, 0)),
                  pl.BlockSpec((1, d, ff), lambda i, e: (e, 0, 0)),
                  pl.BlockSpec((1, ff, d), lambda i, e: (e, 0, 0)),
                  row, row],
        out_specs=pl.BlockSpec((tm, d), lambda i, e: (i, 0)),
        out_shape=jax.ShapeDtypeStruct((m, d), F32),
        scratch_shapes=[pltpu.VMEM((tm, d), BF16), pltpu.VMEM((tm, d), F32)],
        compiler_params=_params("parallel", "arbitrary"),
        name="moe",
    )(x, gates, wg_bf, wu_bf, wd_bf, vec(g), vec(beta))


def _moe_grouped_kernel(cnt_ref, x_ref, gt_ref, gext_ref, wg_ref, wu_ref, wd_ref, g_ref, beta_ref,
                        o_ref, xb_ref, xs_ref, gs_ref, ys_ref, rrow_ref, rcol_ref, before_ref,
                        *, alpha, per, tile):
    c = pl.program_id(0)
    e = pl.program_id(1)
    g = e // per
    j = e % per
    chunk, d = x_ref.shape
    n_exp = gt_ref.shape[0]
    n_tiles = (cnt_ref[c, g] + tile - 1) // tile

    @pl.when((c == 0) & (e == 0))
    def _():
        r = lax.broadcasted_iota(jnp.int32, (chunk, chunk), 0)
        cc = lax.broadcasted_iota(jnp.int32, (chunk, chunk), 1)
        before_ref[0] = jnp.where(r < cc, 1.0, 0.0).astype(BF16)
        before_ref[1] = jnp.where(cc < r, 1.0, 0.0).astype(BF16)

    @pl.when(e == 0)
    def _():
        xb_ref[...] = x_ref[...].astype(BF16)
        o_ref[...] = jnp.zeros(o_ref.shape, F32)
        gt = gt_ref[...]
        row8 = lax.broadcasted_iota(jnp.int32, (8, chunk), 0)
        member = jnp.zeros((8, chunk), F32)
        for grp in range(n_exp // per):
            in_grp = jnp.sum(gt[grp * per:(grp + 1) * per, :], axis=0, keepdims=True) > 0.0
            member = jnp.where((row8 == grp) & in_grp, 1.0, member)
        rank = _dot(member.astype(BF16), before_ref[0])
        rrow_ref[...] = jnp.where(member > 0.0, rank, -1.0)
        lane_l = lax.broadcasted_iota(jnp.int32, (LANES, LANES), 0)
        lane_g = lax.broadcasted_iota(jnp.int32, (LANES, LANES), 1)
        fold = jnp.where((lane_l < n_exp) & (lane_l // per == lane_g), 1.0, 0.0).astype(BF16)
        picked = jnp.where(gext_ref[...].astype(F32) > 0.0, 1.0, 0.0).astype(BF16)
        member_c = jnp.where(_dot(picked, fold) > 0.0, 1.0, 0.0)
        rank_c = _dot(before_ref[1], member_c.astype(BF16))
        rcol_ref[...] = jnp.where(member_c > 0.0, rank_c, -1.0)

    @pl.when(j == 0)
    def _():
        rank_row = rrow_ref[pl.ds(g, 1), :]

        def gather(t, carry):
            rows = pl.ds(pl.multiple_of(t * tile, 16), tile)
            slot = t * tile + lax.broadcasted_iota(jnp.int32, (tile, 1), 0)
            onehot = jnp.where(rank_row == slot.astype(F32), 1.0, 0.0).astype(BF16)
            xs_ref[rows, :] = _dot(onehot, xb_ref[...]).astype(BF16)
            gs_ref[rows, :] = _dot(onehot, gext_ref[...])
            ys_ref[rows, :] = jnp.zeros((tile, d), F32)
            return carry

        lax.fori_loop(0, n_tiles, gather, 0)

    lane = lax.broadcasted_iota(jnp.int32, (tile, LANES), 1)
    gate_lanes = (lane == e) | (lane == e + n_exp)

    def expert(t, carry):
        rows = pl.ds(pl.multiple_of(t * tile, 16), tile)
        xs = xs_ref[rows, :]
        gate = jnp.sum(jnp.where(gate_lanes, gs_ref[rows, :], 0.0), axis=1, keepdims=True)
        hg = _dot(xs, wg_ref[0])
        hu = _dot(xs, wu_ref[0])
        h = hg * jax.nn.sigmoid(hg) * hu * gate
        ys_ref[rows, :] += _dot(h.astype(BF16), wd_ref[0])
        return carry

    lax.fori_loop(0, n_tiles, expert, 0)

    @pl.when(j == per - 1)
    def _():
        lane_c = lax.broadcasted_iota(jnp.int32, (chunk, LANES), 1)
        rank_col = jnp.sum(jnp.where(lane_c == g, rcol_ref[...], 0.0), axis=1, keepdims=True)

        def scatter(t, carry):
            rows = pl.ds(pl.multiple_of(t * tile, 16), tile)
            slot = t * tile + lax.broadcasted_iota(jnp.int32, (1, tile), 1)
            onehot_t = jnp.where(rank_col == slot.astype(F32), 1.0, 0.0).astype(BF16)
            o_ref[...] += _dot(onehot_t, ys_ref[rows, :].astype(BF16))
            return carry

        lax.fori_loop(0, n_tiles, scatter, 0)

    @pl.when(e == pl.num_programs(1) - 1)
    def _():
        o_ref[...] = _layer_norm(alpha * x_ref[...] + o_ref[...], g_ref[...], beta_ref[...])


def _moe_grouped(x, gates_t, wg_bf, wu_bf, wd_bf, g, beta, alpha):
    m, d = x.shape
    n_exp, _, ff = wg_bf.shape
    per = n_exp // N_GROUPS
    chunk = _tile(m, MOE_CHUNK)
    tile = min(chunk, -(-(chunk // N_GROUPS * 9 // 8) // 16) * 16)
    slots = -(-chunk // tile) * tile
    assert tile % 16 == 0 and 2 * n_exp <= LANES and N_GROUPS <= 8
    gates = gates_t.T
    hi = gates.astype(BF16)
    lo = (gates - hi.astype(F32)).astype(BF16)
    gext = jnp.concatenate([hi, lo, jnp.zeros((m, LANES - 2 * n_exp), BF16)], axis=1)
    counts = jnp.sum(jnp.sum(gates.reshape(m // chunk, chunk, N_GROUPS, per), axis=3) > 0,
                     axis=1).astype(jnp.int32)
    vec = lambda v: v.reshape(1, d)
    row = pl.BlockSpec((1, d), lambda i, e, cnt: (0, 0))
    return pl.pallas_call(
        functools.partial(_moe_grouped_kernel, alpha=alpha, per=per, tile=tile),
        grid_spec=pltpu.PrefetchScalarGridSpec(
            num_scalar_prefetch=1,
            grid=(m // chunk, n_exp),
            in_specs=[pl.BlockSpec((chunk, d), lambda i, e, cnt: (i, 0)),
                      pl.BlockSpec((n_exp, chunk), lambda i, e, cnt: (0, i)),
                      pl.BlockSpec((chunk, LANES), lambda i, e, cnt: (i, 0)),
                      pl.BlockSpec((1, d, ff), lambda i, e, cnt: (e, 0, 0)),
                      pl.BlockSpec((1, d, ff), lambda i, e, cnt: (e, 0, 0)),
                      pl.BlockSpec((1, ff, d), lambda i, e, cnt: (e, 0, 0)),
                      row, row],
            out_specs=pl.BlockSpec((chunk, d), lambda i, e, cnt: (i, 0)),
            scratch_shapes=[pltpu.VMEM((chunk, d), BF16),
                            pltpu.VMEM((slots, d), BF16),
                            pltpu.VMEM((slots, LANES), F32),
                            pltpu.VMEM((slots, d), F32),
                            pltpu.VMEM((8, chunk), F32),
                            pltpu.VMEM((chunk, LANES), F32),
                            pltpu.VMEM((2, chunk, chunk), BF16)]),
        out_shape=jax.ShapeDtypeStruct((m, d), F32),
        compiler_params=_params("arbitrary", "arbitrary"),
        name="moe_grouped",
    )(counts, x, gates_t, gext, wg_bf, wu_bf, wd_bf, vec(g), vec(beta))


def _sb_attend(zs, values, u_neg, mask, overlapped=None):
    n = len(zs)
    first_half, out = [], []
    for t in range(n + 1):
        if t < n:
            z = zs[t]() if callable(zs[t]) else zs[t]
            neg_abs = pltpu.bitcast(pltpu.bitcast(z, jnp.uint32) | jnp.uint32(0x80000000), F32)
            soft = jnp.log(1.0 + jnp.exp2(neg_abs)) * LOG2E
            lb = jnp.minimum(z, 0.0) - soft
            nlk = z - lb
            if mask is not None:
                nlk = jnp.where(mask, nlk, 0.0)
            nlk = nlk.astype(BF16)
            first_half.append((nlk, lb, _dot(nlk, u_neg)))
        if t >= 1:
            nlk, lb, later = first_half[t - 1]
            w = jnp.exp2(lb + later)
            if mask is not None:
                w = jnp.where(mask, w, 0.0)
            out.append((values[t - 1](w.astype(BF16)), later[:, 0:1] - nlk[:, 0:1].astype(F32)))
            if overlapped is not None:
                overlapped[t - 1]()
    return out


def _u_neg(size):
    r = lax.broadcasted_iota(jnp.int32, (size, size), 0)
    c = lax.broadcasted_iota(jnp.int32, (size, size), 1)
    return jnp.where(r > c, -1.0, 0.0).astype(BF16)


def _sb_prompt_kernel(bias_ref, q_ref, kt_ref, vt_ref, o_ref, acc_ref, s_ref, z_ref,
                      *, blk, head_dim):
    hp = pl.program_id(0)
    i = pl.program_id(1)
    heads_per = q_ref.shape[1] // head_dim
    q2 = q_ref[...]
    head_of_lane = lax.broadcasted_iota(jnp.int32, (1, q2.shape[1]), 1) // head_dim
    u_neg = _u_neg(blk)
    part_of_lane = lax.broadcasted_iota(jnp.int32, (blk, SB_BIAS_ROWS), 1)
    qs = []
    for h in range(heads_per):
        bias_lanes = jnp.zeros((blk, SB_BIAS_ROWS), F32)
        for part in range(SB_BIAS_PARTS):
            bias_lanes = jnp.where(part_of_lane == part, bias_ref[hp * heads_per + h, part],
                                   bias_lanes)
        qs.append(jnp.concatenate([jnp.where(head_of_lane == h, q2, jnp.zeros_like(q2)),
                                   bias_lanes.astype(BF16)], axis=1))

    def units_of(key_blocks):
        offs = [pl.multiple_of(kb * blk, blk) for kb in key_blocks]
        return [(o, h) for o in offs for h in range(heads_per)]

    def logit_thunks(key_blocks):
        return [functools.partial(lambda o, h: _dot(qs[h], kt_ref[:, pl.ds(o, blk)]), o, h)
                for o, h in units_of(key_blocks)]

    def visit(zs, key_blocks, mask, first, overlapped=None):
        units = units_of(key_blocks)
        values = [functools.partial(lambda o, w: _dot_nt(w, vt_ref[:, pl.ds(o, blk)]), o)
                  for o, _ in units]
        results = _sb_attend(zs, values, u_neg, mask, overlapped)
        for (pv, total), (o, h) in zip(results, units):
            if first:
                acc_ref[h] = pv
                s_ref[h] = total
            else:
                acc_ref[h] += jnp.exp2(s_ref[h]) * pv
                s_ref[h] += total

    r = lax.broadcasted_iota(jnp.int32, (blk, blk), 0)
    c = lax.broadcasted_iota(jnp.int32, (blk, blk), 1)
    visit(logit_thunks([i]), [i], c < r, True)

    done = 0
    run = 1
    while run < SB_KEY_BLOCKS_PER_STEP:
        top = i - done

        @pl.when(i & run != 0)
        def _(top=top, run=run):
            blocks = [top - 1 - j for j in range(run)]
            visit(logit_thunks(blocks), blocks, None, False)

        done = done + (i & run)
        run *= 2

    n_steps = i // SB_KEY_BLOCKS_PER_STEP
    n_units = SB_KEY_BLOCKS_PER_STEP * heads_per

    def step_blocks(step):
        top = i - i % SB_KEY_BLOCKS_PER_STEP - SB_KEY_BLOCKS_PER_STEP * step
        return [jnp.maximum(top - 1 - j, 0) for j in range(SB_KEY_BLOCKS_PER_STEP)]

    def store_logits(u, thunk):
        z_ref[u] = thunk()

    @pl.when(n_steps > 0)
    def _():
        for u, thunk in enumerate(logit_thunks(step_blocks(0))):
            store_logits(u, thunk)

    def body(step, carry):
        refill = [functools.partial(store_logits, u, thunk)
                  for u, thunk in enumerate(logit_thunks(step_blocks(step + 1)))]
        current = [functools.partial(lambda u: z_ref[u], u) for u in range(n_units)]
        visit(current, step_blocks(step), None, False, refill)
        return carry

    lax.fori_loop(0, n_steps, body, 0)
    out = acc_ref[0]
    for h in range(1, heads_per):
        out = jnp.where(head_of_lane == h, acc_ref[h], out)
    o_ref[...] = out


def _sb_prompt(q_bf, kt_bf, vt_bf, bias2, head_dim):
    n, width = q_bf.shape
    n_groups = width // LANES
    blk = _tile(n, 256)
    parts, rest = [], bias2
    for _ in range(SB_BIAS_PARTS):
        part = rest.astype(BF16).astype(F32)
        parts.append(part)
        rest = rest - part
    bias_parts = jnp.stack(parts, axis=1)
    return pl.pallas_call(
        functools.partial(_sb_prompt_kernel, blk=blk, head_dim=head_dim),
        grid=(n_groups, n // blk),
        in_specs=[pl.BlockSpec(memory_space=pltpu.SMEM),
                  pl.BlockSpec((blk, LANES), lambda p, i: (i, p)),
                  pl.BlockSpec((LANES + SB_BIAS_ROWS, n), lambda p, i: (p, 0)),
                  pl.BlockSpec((LANES, n), lambda p, i: (p, 0))],
        out_specs=pl.BlockSpec((blk, LANES), lambda p, i: (i, p)),
        out_shape=jax.ShapeDtypeStruct((n, width), F32),
        scratch_shapes=[pltpu.VMEM((LANES // head_dim, blk, LANES), F32),
                        pltpu.VMEM((LANES // head_dim, blk, 1), F32),
                        pltpu.VMEM((SB_KEY_BLOCKS_PER_STEP * (LANES // head_dim), blk, blk), F32)],
        compiler_params=_params("parallel", "arbitrary"),
        name="sb_prompt",
    )(bias_parts, q_bf, kt_bf, vt_bf)


def _sb_sample_kernel(pt_ref, qbd_ref, knew_ref, vnew_ref, bias_ref, kpool_ref, vpool_ref,
                      o_ref, kbuf, vbuf, sem, pad_k, pad_v, acc_ref,
                      *, n_pages, page, group, sub, t_new):
    b = pl.program_id(0)
    n_steps = n_pages // group

    def copies(step, slot):
        out = []
        for r in range(group):
            phys = pt_ref[b, (n_steps - 1 - step) * group + r]
            out.append(pltpu.make_async_copy(kpool_ref.at[phys], kbuf.at[slot, r], sem.at[0, slot]))
            out.append(pltpu.make_async_copy(vpool_ref.at[phys], vbuf.at[slot, r], sem.at[1, slot]))
        return out

    for ahead in range(min(SB_PAGE_SLOTS - 1, n_steps)):
        for cp in copies(ahead, ahead):
            cp.start()

    qbd = qbd_ref[0]
    bias = bias_ref[...]
    rows = qbd.shape[0]

    pad_k[...] = jnp.zeros(pad_k.shape, F32)
    pad_v[...] = jnp.zeros(pad_v.shape, F32)
    pad_k[0:t_new, :] = knew_ref[0]
    pad_v[0:t_new, :] = vnew_ref[0]
    t_of_row = lax.broadcasted_iota(jnp.int32, (rows, page), 0) % SB_ROWS_PER_HEAD
    s_of_col = lax.broadcasted_iota(jnp.int32, (rows, page), 1)
    (pv, s_after), = _sb_attend([_dot_nt(qbd, pad_k[...].astype(BF16)) + bias],
                                [lambda w: _dot(w, pad_v[...].astype(BF16))], _u_neg(page),
                                s_of_col < t_of_row)
    acc_ref[...] = pv

    per_sub = sub // page
    u_neg = _u_neg(sub)

    def body(step, s_after):
        slot = step % SB_PAGE_SLOTS
        for cp in copies(step, slot):
            cp.wait()

        @pl.when(step + SB_PAGE_SLOTS - 1 < n_steps)
        def _():
            for cp in copies(step + SB_PAGE_SLOTS - 1, (step + SB_PAGE_SLOTS - 1) % SB_PAGE_SLOTS):
                cp.start()

        def keys_of(buf, first):
            return jnp.concatenate([buf[slot, first + j].astype(BF16) for j in range(per_sub)],
                                   axis=1)

        firsts = list(reversed(range(0, group, per_sub)))
        zs = [_dot(qbd, keys_of(kbuf, f)) + bias for f in firsts]
        values = [functools.partial(lambda f, w: _dot_nt(w, keys_of(vbuf, f)), f) for f in firsts]
        upd = jnp.zeros(acc_ref.shape, F32)
        for pv, total in _sb_attend(zs, values, u_neg, None):
            upd = upd + jnp.exp2(s_after) * pv
            s_after = s_after + total
        acc_ref[...] += upd
        return s_after

    lax.fori_loop(0, n_steps, body, s_after)

    width = acc_ref.shape[1]
    n_heads = rows // SB_ROWS_PER_HEAD
    head_dim = width // n_heads
    head_of_row = lax.broadcasted_iota(jnp.int32, (rows, width), 0) // SB_ROWS_PER_HEAD
    head_of_col = lax.broadcasted_iota(jnp.int32, (rows, width), 1) // head_dim
    own = jnp.where(head_of_row == head_of_col, acc_ref[...], 0.0)
    out = own[0:SB_ROWS_PER_HEAD, :]
    for h in range(1, n_heads):
        out = out + own[h * SB_ROWS_PER_HEAD:(h + 1) * SB_ROWS_PER_HEAD, :]
    o_ref[0] = out[0:t_new, :]


def _sb_sample(q, k_new, v_new, bias2, pool_kt, pool_vt, page_table, q_scale):
    b, t_new, width = q.shape
    n_heads = bias2.shape[0]
    head_dim = width // n_heads
    page = pool_kt.shape[2]
    n_pages = page_table.shape[1]
    group = max(g for g in (4, 2, 1) if n_pages % g == 0)
    sub = min(2, group) * page
    rows = n_heads * SB_ROWS_PER_HEAD
    head_of_col = jnp.arange(width) // head_dim
    qpad = jnp.pad(q * q_scale, ((0, 0), (0, SB_ROWS_PER_HEAD - t_new), (0, 0)))
    qbd = jnp.where(head_of_col[None, None, None, :] == jnp.arange(n_heads)[None, :, None, None],
                    qpad[:, None, :, :], 0.0).reshape(b, rows, width).astype(BF16)
    bias_rows = jnp.repeat(bias2, SB_ROWS_PER_HEAD).reshape(rows, 1)
    return pl.pallas_call(
        functools.partial(_sb_sample_kernel, n_pages=n_pages, page=page, group=group, sub=sub,
                          t_new=t_new),
        grid_spec=pltpu.PrefetchScalarGridSpec(
            num_scalar_prefetch=1,
            grid=(b,),
            in_specs=[pl.BlockSpec((1, rows, width), lambda i, pt: (i, 0, 0)),
                      pl.BlockSpec((1, t_new, width), lambda i, pt: (i, 0, 0)),
                      pl.BlockSpec((1, t_new, width), lambda i, pt: (i, 0, 0)),
                      pl.BlockSpec((rows, 1), lambda i, pt: (0, 0)),
                      pl.BlockSpec(memory_space=pl.ANY),
                      pl.BlockSpec(memory_space=pl.ANY)],
            out_specs=pl.BlockSpec((1, t_new, width), lambda i, pt: (i, 0, 0)),
            scratch_shapes=[pltpu.VMEM((SB_PAGE_SLOTS, group, width, page), F32),
                            pltpu.VMEM((SB_PAGE_SLOTS, group, width, page), F32),
                            pltpu.SemaphoreType.DMA((2, SB_PAGE_SLOTS)),
                            pltpu.VMEM((page, width), F32),
                            pltpu.VMEM((page, width), F32),
                            pltpu.VMEM((rows, width), F32)]),
        out_shape=jax.ShapeDtypeStruct((b, t_new, width), F32),
        compiler_params=_params("arbitrary"),
        name="sb_sample",
    )(page_table, qbd, k_new, v_new, bias_rows, pool_kt, pool_vt)


def _feature_major(t):
    lead = t.shape[:-3]
    tokens, heads, dim = t.shape[-3:]
    n = len(lead)
    return jnp.transpose(t, (*range(n), n + 1, n + 2, n)).reshape(*lead, heads * dim, tokens)


def _token_major(t, heads):
    lead = t.shape[:-2]
    width, tokens = t.shape[-2:]
    n = len(lead)
    t = t.reshape(*lead, heads, width // heads, tokens)
    return jnp.transpose(t, (*range(n), n + 2, n, n + 1))


def kernel(x_prompt, x_sample, state_conv, cache_sb_k, cache_sb_v, cache_mem_k, cache_mem_v, page_table, mem_prompt, a_w_in, a_b_in, a_conv_w, a_conv_b, a_norm_g, a_norm_b, a_w_out, a_b_out, b_w_in, b_b_in, b_sb_bias, b_w_out, b_b_out, w_mem_kv, ln_mix_g, ln_mix_b, ln_ffn_g, ln_ffn_b, w_router, b_router, w_exp_gate, w_exp_up, w_exp_down):
    depth = w_mem_kv.shape[0]
    n_mixers = 2
    alpha = (2.0 * depth) ** 0.25
    bp, seq, d = x_prompt.shape
    bs, t_new, _ = x_sample.shape
    mem_heads, mem_hd = cache_mem_k.shape[-2:]
    mem_w = mem_heads * mem_hd
    sb_heads, sb_hd = cache_sb_k.shape[-2:]
    sb_w = sb_heads * sb_hd
    conv_ch = a_conv_w.shape[-1]
    mem_tokens = mem_prompt.shape[1]
    assert bp == 1
    sb_q_scale = float(sb_hd) ** -0.5 * LOG2E

    y_p = x_prompt.reshape(seq, d)
    y_s = x_sample.reshape(bs * t_new, d)
    mem_p = mem_prompt.reshape(mem_tokens, d)
    conv_p, conv_s, sbk_p, sbv_p, sbk_s, sbv_s, memk_p, memv_p = ([] for _ in range(8))

    for layer in range(depth):
        kvt = _linear_t(mem_p, w_mem_kv[layer].T.astype(BF16), name="mem_kv")
        mkt_p = kvt[:mem_w][None]
        mvt_p = kvt[mem_w:][None]
        memk_p.append(_token_major(mkt_p, mem_heads))
        memv_p.append(_token_major(mvt_p, mem_heads))
        mkt_s = _feature_major(cache_mem_k[layer])
        mvt_s = _feature_major(cache_mem_v[layer])
        i = layer // n_mixers
        if layer % n_mixers == 0:
            w_in = a_w_in[i].astype(BF16)
            proj_p = _linear(y_p, w_in, a_b_in[i], name="conv_in_proj")
            proj_s = _linear(y_s, w_in, a_b_in[i], name="conv_in_proj_s")
            mix_p, st_p = _conv_prompt(proj_p, a_conv_w[i], a_conv_b[i], a_norm_g[i], a_norm_b[i])
            conv_p.append(st_p[None])
            proj_tm = jnp.transpose(proj_s.reshape(bs, t_new, -1), (1, 0, 2))
            mix_s, st_s = _conv_sample(proj_tm, jnp.transpose(state_conv[i], (1, 0, 2)),
                                       a_conv_w[i], a_conv_b[i], a_norm_g[i], a_norm_b[i])
            mix_s = jnp.transpose(mix_s, (1, 0, 2)).reshape(bs * t_new, conv_ch)
            conv_s.append(jnp.transpose(st_s, (1, 0, 2)))
            q_col = (2 * conv_ch) // mem_w
            assert q_col * mem_w == 2 * conv_ch
            qm_p, qm_s = proj_p[None], proj_s.reshape(bs, t_new, -1)
            w_out, b_out = a_w_out[i], a_b_out[i]
        else:
            bias2 = b_sb_bias[i] * LOG2E
            q_bf, kt, vt, kt_bf, vt_bf, qm = _sb_in_proj(y_p, b_w_in[i], b_b_in[i], sb_w, sb_q_scale)
            sbk_p.append(_token_major(kt[None], sb_heads))
            sbv_p.append(_token_major(vt[None], sb_heads))
            mix_p = _sb_prompt(q_bf, kt_bf, vt_bf, bias2, sb_hd)
            proj_s = _linear(y_s, b_w_in[i].astype(BF16), b_b_in[i], name="sb_in_proj_s")
            proj_s3 = proj_s.reshape(bs, t_new, -1)
            k_s = proj_s3[:, :, sb_w:2 * sb_w]
            v_s = proj_s3[:, :, 2 * sb_w:3 * sb_w]
            sbk_s.append(k_s.reshape(bs, t_new, sb_heads, sb_hd))
            sbv_s.append(v_s.reshape(bs, t_new, sb_heads, sb_hd))
            mix_s = _sb_sample(proj_s3[:, :, :sb_w], k_s, v_s, bias2,
                               _feature_major(cache_sb_k[i]), _feature_major(cache_sb_v[i]),
                               page_table, sb_q_scale)
            mix_s = mix_s.reshape(bs * t_new, sb_w)
            q_col = (3 * sb_w) // mem_w
            assert q_col * mem_w == 3 * sb_w
            qm_p, qm_s = qm[None], proj_s3
            q_col_p = 0
            w_out, b_out = b_w_out[i], b_b_out[i]

        mem_out_p = _mem_attn(qm_p, q_col if layer % n_mixers == 0 else q_col_p, mkt_p, mvt_p,
                              mem_heads)
        mem_out_s = _mem_attn(qm_s, q_col, mkt_s, mvt_s, mem_heads)
        w_out_bf = w_out.astype(BF16)
        y_p = _out_proj(y_p, mix_p, mem_out_p.reshape(seq, mem_w), w_out_bf, b_out,
                        ln_mix_g[layer], ln_mix_b[layer], alpha)
        y_s = _out_proj(y_s, mix_s, mem_out_s.reshape(bs * t_new, mem_w), w_out_bf, b_out,
                        ln_mix_g[layer], ln_mix_b[layer], alpha)

        wg, wu, wd = (w[layer].astype(BF16) for w in (w_exp_gate, w_exp_up, w_exp_down))
        y_p = _moe_grouped(y_p, _router(y_p, w_router, b_router), wg, wu, wd,
                           ln_ffn_g[layer], ln_ffn_b[layer], alpha)
        y_s = _moe(y_s, _router(y_s, w_router, b_router).T, wg, wu, wd,
                   ln_ffn_g[layer], ln_ffn_b[layer], alpha)

    return (y_p.reshape(bp, seq, d), y_s.reshape(bs, t_new, d),
            jnp.stack(conv_p), jnp.stack(conv_s), jnp.stack(sbk_p), jnp.stack(sbv_p),
            jnp.stack(sbk_s), jnp.stack(sbv_s), jnp.stack(memk_p), jnp.stack(memv_p))
```

```python
import functools
import math

import jax
import jax.numpy as jnp
from jax import lax
from jax.experimental import pallas as pl
from jax.experimental.pallas import tpu as pltpu

LN_EPS = 1e-5
N_GROUPS = 4
TOP_K = 2
CONV_HALO = 32
LANES = 128
MATMUL_ROWS = 512
SB_ROWS_PER_HEAD = 8
SB_KEY_BLOCKS_PER_STEP = 8
SB_PAGE_SLOTS = 3
SB_BIAS_ROWS = 16
SB_BIAS_PARTS = 3
MOE_CHUNK = 1024
MOE_EXPERTS_PER_STEP = 2
VMEM_LIMIT = 48 * 1024 * 1024
LOG2E = math.log2(math.e)

BF16 = jnp.bfloat16
F32 = jnp.float32


def _tile(n, pref):
    if n <= pref:
        return n
    t = pref - pref % 8
    while t >= 8:
        if n % t == 0:
            return t
        t -= 8
    return n


def _params(*sem):
    return pltpu.CompilerParams(dimension_semantics=sem, vmem_limit_bytes=VMEM_LIMIT)


def _layer_norm(x, g, b):
    mu = jnp.mean(x, axis=-1, keepdims=True)
    xc = x - mu
    var = jnp.mean(xc * xc, axis=-1, keepdims=True)
    return xc * lax.rsqrt(var + LN_EPS) * g + b


def _dot(a, b):
    return jnp.dot(a, b, preferred_element_type=F32)


def _dot_nt(a, b):
    return lax.dot_general(a, b, (((1,), (1,)), ((), ())), preferred_element_type=F32)


def _linear_kernel(x_ref, w_ref, b_ref, o_ref):
    o_ref[...] = _dot(x_ref[...].astype(BF16), w_ref[...]) + b_ref[...]


def _linear(x, w_bf, b, *, name="linear"):
    m, k = x.shape
    n = w_bf.shape[1]
    tm = _tile(m, MATMUL_ROWS)
    return pl.pallas_call(
        _linear_kernel,
        grid=(m // tm,),
        in_specs=[pl.BlockSpec((tm, k), lambda i: (i, 0)),
                  pl.BlockSpec((k, n), lambda i: (0, 0)),
                  pl.BlockSpec((1, n), lambda i: (0, 0))],
        out_specs=pl.BlockSpec((tm, n), lambda i: (i, 0)),
        out_shape=jax.ShapeDtypeStruct((m, n), F32),
        compiler_params=_params("parallel"),
        name=name,
    )(x, w_bf, b.reshape(1, n))


def _linear_t_kernel(x_ref, wt_ref, o_ref):
    o_ref[...] = _dot_nt(wt_ref[...], x_ref[...].astype(BF16))


def _linear_t(x, wt_bf, *, name="linear_t"):
    m, k = x.shape
    n = wt_bf.shape[0]
    tm = _tile(m, 256)
    return pl.pallas_call(
        _linear_t_kernel,
        grid=(m // tm,),
        in_specs=[pl.BlockSpec((tm, k), lambda i: (i, 0)),
                  pl.BlockSpec((n, k), lambda i: (0, 0))],
        out_specs=pl.BlockSpec((n, tm), lambda i: (0, i)),
        out_shape=jax.ShapeDtypeStruct((n, m), F32),
        compiler_params=_params("parallel"),
        name=name,
    )(x, wt_bf)


def _sb_in_proj_kernel(x_ref, wq_ref, wkt_ref, wvt_ref, wm_ref, bq_ref, bk_ref, bv_ref, bm_ref,
                       q_ref, kt_ref, vt_ref, ktb_ref, vtb_ref, qm_ref, *, q_scale):
    xb = x_ref[...].astype(BF16)
    q_ref[...] = ((_dot(xb, wq_ref[...]) + bq_ref[...]) * q_scale).astype(BF16)
    kt = _dot_nt(wkt_ref[...], xb) + bk_ref[...]
    kt_ref[...] = kt
    ones = jnp.ones((SB_BIAS_ROWS, kt.shape[1]), BF16)
    for p in range(kt.shape[0] // LANES):
        base = p * (LANES + SB_BIAS_ROWS)
        ktb_ref[base:base + LANES, :] = kt[p * LANES:(p + 1) * LANES].astype(BF16)
        ktb_ref[base + LANES:base + LANES + SB_BIAS_ROWS, :] = ones
    vt = _dot_nt(wvt_ref[...], xb) + bv_ref[...]
    vt_ref[...] = vt
    vtb_ref[...] = vt.astype(BF16)
    qm_ref[...] = _dot(xb, wm_ref[...]) + bm_ref[...]


def _sb_in_proj(x, w_in, b_in, sb_w, q_scale):
    n, d = x.shape
    wm = w_in.shape[1] - 3 * sb_w
    tm = _tile(n, MATMUL_ROWS)
    kt_rows =sb_w // LANES * (LANES + SB_BIAS_ROWS)
    wq = w_in[:, :sb_w].astype(BF16)
    wkt = w_in[:, sb_w:2 * sb_w].T.astype(BF16)
    wvt = w_in[:, 2 * sb_w:3 * sb_w].T.astype(BF16)
    wmem = w_in[:, 3 * sb_w:].astype(BF16)
    const = lambda shape: pl.BlockSpec(shape, lambda i: (0, 0))
    return pl.pallas_call(
        functools.partial(_sb_in_proj_kernel, q_scale=q_scale),
        grid=(n // tm,),
        in_specs=[pl.BlockSpec((tm, d), lambda i: (i, 0)),
                  const((d, sb_w)), const((sb_w, d)), const((sb_w, d)), const((d, wm)),
                  const((1, sb_w)), const((sb_w, 1)), const((sb_w, 1)), const((1, wm))],
        out_specs=[pl.BlockSpec((tm, sb_w), lambda i: (i, 0)),
                   pl.BlockSpec((sb_w, tm), lambda i: (0, i)),
                   pl.BlockSpec((sb_w, tm), lambda i: (0, i)),
                   pl.BlockSpec((kt_rows, tm), lambda i: (0, i)),
                   pl.BlockSpec((sb_w, tm), lambda i: (0, i)),
                   pl.BlockSpec((tm, wm), lambda i: (i, 0))],
        out_shape=[jax.ShapeDtypeStruct((n, sb_w), BF16),
                   jax.ShapeDtypeStruct((sb_w, n), F32),
                   jax.ShapeDtypeStruct((sb_w, n), F32),
                   jax.ShapeDtypeStruct((kt_rows, n), BF16),
                   jax.ShapeDtypeStruct((sb_w, n), BF16),
                   jax.ShapeDtypeStruct((n, wm), F32)],
        compiler_params=_params("parallel"),
        name="sb_in_proj",
    )(x, wq, wkt, wvt, wmem,
      b_in[:sb_w].reshape(1, sb_w), b_in[sb_w:2 * sb_w].reshape(sb_w, 1),
      b_in[2 * sb_w:3 * sb_w].reshape(sb_w, 1), b_in[3 * sb_w:].reshape(1, wm))


def _conv_prompt_kernel(a_ref, gate_ref, w_ref, cb_ref, ng_ref, nb_ref,
                        mix_ref, st_ref, f_ref, g_ref, *, tile, width, chunk):
    i = pl.program_id(0)
    halo = CONV_HALO
    sub = 8

    @pl.when(i == 0)
    def _():
        f_ref[0:halo, :] = jnp.zeros((halo, f_ref.shape[1]), F32)

    f_ref[halo:halo + tile, :] = a_ref[...] * jax.nn.sigmoid(gate_ref[...])
    shifted_rows = g_ref.shape[1]
    for b in range(1, sub):
        g_ref[b - 1] = f_ref[b:b + shifted_rows, :]
    first = halo - (width - 1)
    for c in range(tile // chunk):
        y = jnp.zeros((chunk, f_ref.shape[1]), F32) + cb_ref[...]
        for k in range(width):
            b = (first + k) % sub
            row = c * chunk + (first + k) - b
            src = f_ref[row:row + chunk, :] if b == 0 else g_ref[b - 1, row:row + chunk, :]
            y = y + src * w_ref[k:k + 1, :]
        y = _layer_norm(y, ng_ref[...], nb_ref[...])
        mix_ref[c * chunk:(c + 1) * chunk, :] = y * jax.nn.sigmoid(y)

    @pl.when(i == pl.num_programs(0) - 1)
    def _():
        st_ref[...] = f_ref[halo + tile - (width - 1):halo + tile, :]

    f_ref[0:halo, :] = f_ref[tile:tile + halo, :]


def _conv_prompt(proj, conv_w, conv_b, norm_g, norm_b):
    n = proj.shape[0]
    width, c = conv_w.shape
    tile = _tile(n, 256)
    chunk = _tile(tile, 32)
    vec = lambda v: v.reshape(1, c)
    row = pl.BlockSpec((1, c), lambda i: (0, 0))
    return pl.pallas_call(
        functools.partial(_conv_prompt_kernel, tile=tile, width=width, chunk=chunk),
        grid=(n // tile,),
        in_specs=[pl.BlockSpec((tile, c), lambda i: (i, 0)),
                  pl.BlockSpec((tile, c), lambda i: (i, 1)),
                  pl.BlockSpec((width, c), lambda i: (0, 0)),
                  row, row, row],
        out_specs=[pl.BlockSpec((tile, c), lambda i: (i, 0)),
                   pl.BlockSpec((width - 1, c), lambda i: (0, 0))],
        out_shape=[jax.ShapeDtypeStruct((n, c), F32),
                   jax.ShapeDtypeStruct((width - 1, c), F32)],
        scratch_shapes=[pltpu.VMEM((tile + CONV_HALO, c), F32),
                        pltpu.VMEM((7, tile + CONV_HALO - 8, c), F32)],
        compiler_params=_params("arbitrary"),
        name="conv_prompt",
    )(proj, proj, conv_w, vec(conv_b), vec(norm_g), vec(norm_b))


def _conv_sample_kernel(a_ref, gate_ref, past_ref, w_ref, cb_ref, ng_ref, nb_ref,
                        mix_ref, st_ref, *, t_new, width):
    hist = width - 1
    new = [a_ref[t] * jax.nn.sigmoid(gate_ref[t]) for t in range(t_new)]
    full = lambda j: past_ref[j] if j < hist else new[j - hist]
    for t in range(t_new):
        y = jnp.zeros(new[0].shape, F32) + cb_ref[...]
        for k in range(width):
            y = y + full(t + k) * w_ref[k:k + 1, :]
        y = _layer_norm(y, ng_ref[...], nb_ref[...])
        mix_ref[t] = y * jax.nn.sigmoid(y)
    for j in range(hist):
        st_ref[j] = full(j + t_new)


def _conv_sample(proj_tm, past_tm, conv_w, conv_b, norm_g, norm_b):
    t_new, batch, _ = proj_tm.shape
    width, c = conv_w.shape
    vec = lambda v: v.reshape(1, c)
    row = pl.BlockSpec((1, c), lambda i: (0, 0))
    return pl.pallas_call(
        functools.partial(_conv_sample_kernel, t_new=t_new, width=width),
        grid=(1,),
        in_specs=[pl.BlockSpec((t_new, batch, c), lambda i: (0, 0, 0)),
                  pl.BlockSpec((t_new, batch, c), lambda i: (0, 0, 1)),
                  pl.BlockSpec((width - 1, batch, c), lambda i: (0, 0, 0)),
                  pl.BlockSpec((width, c), lambda i: (0, 0)),
                  row, row, row],
        out_specs=[pl.BlockSpec((t_new, batch, c), lambda i: (0, 0, 0)),
                   pl.BlockSpec((width - 1, batch, c), lambda i: (0, 0, 0))],
        out_shape=[jax.ShapeDtypeStruct((t_new, batch, c), F32),
                   jax.ShapeDtypeStruct((width - 1, batch, c), F32)],
        compiler_params=_params("arbitrary"),
        name="conv_sample",
    )(proj_tm, proj_tm, past_tm, conv_w, vec(conv_b), vec(norm_g), vec(norm_b))


def _mem_attn_kernel(q_ref, kt_ref, vt_ref, o_ref, *, heads, scale):
    q = q_ref[0]
    kt = kt_ref[0].astype(BF16)
    vt = vt_ref[0].astype(BF16)
    width = q.shape[-1]
    head_of_lane = lax.broadcasted_iota(jnp.int32, (1, width), 1) // (width // heads)
    out = jnp.zeros(q.shape, F32)
    for h in range(heads):
        sel = head_of_lane == h
        s = _dot(jnp.where(sel, q, 0.0).astype(BF16), kt) * scale
        p = jnp.exp(s - jnp.max(s, axis=-1, keepdims=True))
        p = p / jnp.sum(p, axis=-1, keepdims=True)
        out = out + jnp.where(sel, _dot_nt(p.astype(BF16), vt), 0.0)
    o_ref[0] = out


def _mem_attn(q_src, col_block, mem_kt, mem_vt, heads):
    b, t, _ = q_src.shape
    width, m = mem_kt.shape[1:]
    tq = _tile(t, 512)
    scale = float(width // heads) ** -0.5
    return pl.pallas_call(
        functools.partial(_mem_attn_kernel, heads=heads, scale=scale),
        grid=(b, t // tq),
        in_specs=[pl.BlockSpec((1, tq, width), lambda i, j: (i, j, col_block)),
                  pl.BlockSpec((1, width, m), lambda i, j: (i, 0, 0)),
                  pl.BlockSpec((1, width, m), lambda i, j: (i, 0, 0))],
        out_specs=pl.BlockSpec((1, tq, width), lambda i, j: (i, j, 0)),
        out_shape=jax.ShapeDtypeStruct((b, t, width), F32),
        compiler_params=_params("parallel", "parallel"),
        name="mem_attn",
    )(q_src, mem_kt, mem_vt)


def _out_proj_kernel(x_ref, mix_ref, mem_ref, wa_ref, wb_ref, b_ref, g_ref, beta_ref,
                     o_ref, *, alpha):
    acc = _dot(mix_ref[...].astype(BF16), wa_ref[...])
    acc = acc + _dot(mem_ref[...].astype(BF16), wb_ref[...]) + b_ref[...]
    o_ref[...] = _layer_norm(alpha * x_ref[...] + acc, g_ref[...], beta_ref[...])


def _out_proj(x, mix, mem, w_out_bf, b_out, g, beta, alpha):
    m, d = x.shape
    cm = mix.shape[1]
    cw = mem.shape[1]
    tm = _tile(m, MATMUL_ROWS)
    vec = lambda v: v.reshape(1, d)
    row = pl.BlockSpec((1, d), lambda i: (0, 0))
    return pl.pallas_call(
        functools.partial(_out_proj_kernel, alpha=alpha),
        grid=(m // tm,),
        in_specs=[pl.BlockSpec((tm, d), lambda i: (i, 0)),
                  pl.BlockSpec((tm, cm), lambda i: (i, 0)),
                  pl.BlockSpec((tm, cw), lambda i: (i, 0)),
                  pl.BlockSpec((cm, d), lambda i: (0, 0)),
                  pl.BlockSpec((cw, d), lambda i: (0, 0)),
                  row, row, row],
        out_specs=pl.BlockSpec((tm, d), lambda i: (i, 0)),
        out_shape=jax.ShapeDtypeStruct((m, d), F32),
        compiler_params=_params("parallel"),
        name="out_proj_ln",
    )(x, mix, mem, w_out_bf[:cm], w_out_bf[cm:], vec(b_out), vec(g), vec(beta))


def _top2(rows):
    best, bi = rows[0], jnp.zeros(rows[0].shape, jnp.int32)
    for j in range(1, len(rows)):
        upd = rows[j] > best
        best = jnp.where(upd, rows[j], best)
        bi = jnp.where(upd, j, bi)
    sec, si = None, None
    for j in range(len(rows)):
        cand = jnp.where(bi == j, -jnp.inf, rows[j])
        if sec is None:
            sec, si = cand, jnp.zeros(cand.shape, jnp.int32)
        else:
            upd = cand > sec
            sec = jnp.where(upd, cand, sec)
            si = jnp.where(upd, j, si)
    return best, sec, bi, si


def _router_kernel(x_ref, wt_ref, b_ref, o_ref, *, n_groups):
    logits = lax.dot_general(wt_ref[...], x_ref[...], (((1,), (1,)), ((), ())),
                             preferred_element_type=F32,
                             precision=lax.Precision.HIGHEST) + b_ref[...]
    n_exp = logits.shape[0]
    per = n_exp // n_groups
    e = jnp.exp(logits - jnp.max(logits, axis=0, keepdims=True))
    probs = e / jnp.sum(e, axis=0, keepdims=True)
    picks = [_top2([probs[g * per + j:g * per + j + 1, :] for j in range(per)])
             for g in range(n_groups)]
    score = picks[0][0] + picks[0][1]
    sel = picks[0]
    g_sel = jnp.zeros(score.shape, jnp.int32)
    for g in range(1, n_groups):
        sc = picks[g][0] + picks[g][1]
        upd = sc > score
        score = jnp.where(upd, sc, score)
        g_sel = jnp.where(upd, g, g_sel)
        sel = tuple(jnp.where(upd, a, b) for a, b in zip(picks[g], sel))
    p1, p2, i1, i2 = sel
    denom = p1 + p2
    e1 = g_sel * per + i1
    e2 = g_sel * per + i2
    eidx = lax.broadcasted_iota(jnp.int32, logits.shape, 0)
    o_ref[...] = (jnp.where(eidx == e1, p1 / denom, 0.0)
                  + jnp.where(eidx == e2, p2 / denom, 0.0))


def _router(x, w_router, b_router):
    m, d = x.shape
    n_exp = w_router.shape[1]
    tm = _tile(m, 512)
    return pl.pallas_call(
        functools.partial(_router_kernel, n_groups=N_GROUPS),
        grid=(m // tm,),
        in_specs=[pl.BlockSpec((tm, d), lambda i: (i, 0)),
                  pl.BlockSpec((n_exp, d), lambda i: (0, 0)),
                  pl.BlockSpec((n_exp, 1), lambda i: (0, 0))],
        out_specs=pl.BlockSpec((n_exp, tm), lambda i: (0, i)),
        out_shape=jax.ShapeDtypeStruct((n_exp, m), F32),
        compiler_params=_params("parallel"),
        name="router",
    )(x, w_router.T, b_router.reshape(n_exp, 1))


def _moe_kernel(x_ref, gates_ref, wg_ref, wu_ref, wd_ref, g_ref, beta_ref, o_ref,
                xb_ref, acc_ref, *, alpha):
    e = pl.program_id(1)

    @pl.when(e == 0)
    def _():
        xb_ref[...] = x_ref[...].astype(BF16)
        acc_ref[...] = jnp.zeros(acc_ref.shape, F32)

    gates = gates_ref[...]
    lane = lax.broadcasted_iota(jnp.int32, gates.shape, 1)
    gate = jnp.sum(jnp.where(lane == e, gates, 0.0), axis=1, keepdims=True)
    xb = xb_ref[...]
    hg = _dot(xb, wg_ref[0])
    hu = _dot(xb, wu_ref[0])
    h = hg * jax.nn.sigmoid(hg) * hu * gate
    acc_ref[...] += _dot(h.astype(BF16), wd_ref[0])

    @pl.when(e == pl.num_programs(1) - 1)
    def _():
        o_ref[...] = _layer_norm(alpha * x_ref[...] + acc_ref[...], g_ref[...], beta_ref[...])


def _moe(x, gates, wg_bf, wu_bf, wd_bf, g, beta, alpha):
    m, d = x.shape
    n_exp, _, ff = wg_bf.shape
    tm = _tile(m, 512)
    vec = lambda v: v.reshape(1, d)
    row = pl.BlockSpec((1, d), lambda i, e: (0, 0))
    return pl.pallas_call(
        functools.partial(_moe_kernel, alpha=alpha),
        grid=(m // tm, n_exp),
        in_specs=[pl.BlockSpec((tm, d), lambda i, e: (i, 0)),
                  pl.BlockSpec((tm, n_exp), lambda i, e: (i, 0)),
                  pl.BlockSpec((1, d, ff), lambda i, e: (e, 0, 0)),
                  pl.BlockSpec((1, d, ff), lambda i, e: (e, 0, 0)),
                  pl.BlockSpec((1, ff, d), lambda i, e: (e, 0, 0)),
                  row, row],
        out_specs=pl.BlockSpec((tm, d), lambda i, e: (i, 0)),
        out_shape=jax.ShapeDtypeStruct((m, d), F32),
        scratch_shapes=[pltpu.VMEM((tm, d), BF16), pltpu.VMEM((tm, d), F32)],
        compiler_params=_params("parallel", "arbitrary"),
        name="moe",
    )(x, gates, wg_bf, wu_bf, wd_bf, vec(g), vec(beta))


def _moe_grouped_kernel(cnt_ref, x_ref, gt_ref, gext_ref, wg_ref, wu_ref, wd_ref, g_ref, beta_ref,
                        o_ref, xb_ref, xs_ref, gs_ref, ys_ref, rrow_ref, rcol_ref, before_ref,
                        *, alpha, per, tile):
    c = pl.program_id(0)
    step = pl.program_id(1)
    e = step * MOE_EXPERTS_PER_STEP
    g = e // per
    j = e % per
    chunk, d = x_ref.shape
    n_exp = gt_ref.shape[0]
    n_tiles = (cnt_ref[c, g] + tile - 1) // tile

    @pl.when((c == 0) & (e == 0))
    def _():
        r = lax.broadcasted_iota(jnp.int32, (chunk, chunk), 0)
        cc = lax.broadcasted_iota(jnp.int32, (chunk, chunk), 1)
        before_ref[0] = jnp.where(r < cc, 1.0, 0.0).astype(BF16)
        before_ref[1] = jnp.where(cc < r, 1.0, 0.0).astype(BF16)

    @pl.when(e == 0)
    def _():
        xb_ref[...] = x_ref[...].astype(BF16)
        o_ref[...] = jnp.zeros(o_ref.shape, F32)
        gt = gt_ref[...]
        row8 = lax.broadcasted_iota(jnp.int32, (8, chunk), 0)
        member = jnp.zeros((8, chunk), F32)
        for grp in range(n_exp // per):
            in_grp = jnp.sum(gt[grp * per:(grp + 1) * per, :], axis=0, keepdims=True) > 0.0
            member = jnp.where((row8 == grp) & in_grp, 1.0, member)
        rank = _dot(member.astype(BF16), before_ref[0])
        rrow_ref[...] = jnp.where(member > 0.0, rank, -1.0)
        lane_l = lax.broadcasted_iota(jnp.int32, (LANES, LANES), 0)
        lane_g = lax.broadcasted_iota(jnp.int32, (LANES, LANES), 1)
        fold = jnp.where((lane_l < n_exp) & (lane_l // per == lane_g), 1.0, 0.0).astype(BF16)
        picked = jnp.where(gext_ref[...].astype(F32) > 0.0, 1.0, 0.0).astype(BF16)
        member_c = jnp.where(_dot(picked, fold) > 0.0, 1.0, 0.0)
        rank_c = _dot(before_ref[1], member_c.astype(BF16))
        rcol_ref[...] = jnp.where(member_c > 0.0, rank_c, -1.0)

    @pl.when(j == 0)
    def _():
        rank_row = rrow_ref[pl.ds(g, 1), :]

        def gather(t, carry):
            rows = pl.ds(pl.multiple_of(t * tile, 16), tile)
            slot = t * tile + lax.broadcasted_iota(jnp.int32, (tile, 1), 0)
            onehot = jnp.where(rank_row == slot.astype(F32), 1.0, 0.0).astype(BF16)
            xs_ref[rows, :] = _dot(onehot, xb_ref[...]).astype(BF16)
            gs_ref[rows, :] = _dot(onehot, gext_ref[...])
            ys_ref[rows, :] = jnp.zeros((tile, d), F32)
            return carry

        lax.fori_loop(0, n_tiles, gather, 0)

    lane = lax.broadcasted_iota(jnp.int32, (tile, LANES), 1)

    def experts(t, carry):
        rows = pl.ds(pl.multiple_of(t * tile, 16), tile)
        xs = xs_ref[rows, :]
        gs = gs_ref[rows, :]
        out = jnp.zeros((tile, d), F32)
        for k in range(MOE_EXPERTS_PER_STEP):
            gate_lanes = (lane == e + k) | (lane == e + k + n_exp)
            gate = jnp.sum(jnp.where(gate_lanes, gs, 0.0), axis=1, keepdims=True)
            hg = _dot(xs, wg_ref[k])
            hu = _dot(xs, wu_ref[k])
            h = hg * jax.nn.sigmoid(hg) * hu * gate
            out = out + _dot(h.astype(BF16), wd_ref[k])
        ys_ref[rows, :] += out
        return carry

    lax.fori_loop(0, n_tiles, experts, 0)

    @pl.when(j + MOE_EXPERTS_PER_STEP == per)
    def _():
        lane_c = lax.broadcasted_iota(jnp.int32, (chunk, LANES), 1)
        rank_col = jnp.sum(jnp.where(lane_c == g, rcol_ref[...], 0.0), axis=1, keepdims=True)

        def scatter(t, carry):
            rows = pl.ds(pl.multiple_of(t * tile, 16), tile)
            slot = t * tile + lax.broadcasted_iota(jnp.int32, (1, tile), 1)
            onehot_t = jnp.where(rank_col == slot.astype(F32), 1.0, 0.0).astype(BF16)
            o_ref[...] += _dot(onehot_t, ys_ref[rows, :].astype(BF16))
            return carry

        lax.fori_loop(0, n_tiles, scatter, 0)

    @pl.when(step == pl.num_programs(1) - 1)
    def _():
        o_ref[...] = _layer_norm(alpha * x_ref[...] + o_ref[...], g_ref[...], beta_ref[...])


def _moe_grouped(x, gates_t, wg_bf, wu_bf, wd_bf, g, beta, alpha):
    m, d = x.shape
    n_exp, _, ff = wg_bf.shape
    per = n_exp // N_GROUPS
    chunk = _tile(m, MOE_CHUNK)
    tile = min(chunk, -(-(chunk // N_GROUPS * 9 // 8) // 16) * 16)
    slots = -(-chunk // tile) * tile
    eps = MOE_EXPERTS_PER_STEP
    assert tile % 16 == 0 and 2 * n_exp <= LANES and N_GROUPS <= 8 and per % eps == 0
    gates = gates_t.T
    hi = gates.astype(BF16)
    lo = (gates - hi.astype(F32)).astype(BF16)
    gext = jnp.concatenate([hi, lo, jnp.zeros((m, LANES - 2 * n_exp), BF16)], axis=1)
    counts = jnp.sum(jnp.sum(gates.reshape(m // chunk, chunk, N_GROUPS, per), axis=3) > 0,
                     axis=1).astype(jnp.int32)
    vec = lambda v: v.reshape(1, d)
    row = pl.BlockSpec((1, d), lambda i, e, cnt: (0, 0))
    return pl.pallas_call(
        functools.partial(_moe_grouped_kernel, alpha=alpha, per=per, tile=tile),
        grid_spec=pltpu.PrefetchScalarGridSpec(
            num_scalar_prefetch=1,
            grid=(m // chunk, n_exp // eps),
            in_specs=[pl.BlockSpec((chunk, d), lambda i, e, cnt: (i, 0)),
                      pl.BlockSpec((n_exp, chunk), lambda i, e, cnt: (0, i)),
                      pl.BlockSpec((chunk, LANES), lambda i, e, cnt: (i, 0)),
                      pl.BlockSpec((eps, d, ff), lambda i, e, cnt: (e, 0, 0)),
                      pl.BlockSpec((eps, d, ff), lambda i, e, cnt: (e, 0, 0)),
                      pl.BlockSpec((eps, ff, d), lambda i, e, cnt: (e, 0, 0)),
                      row, row],
            out_specs=pl.BlockSpec((chunk, d), lambda i, e, cnt: (i, 0)),
            scratch_shapes=[pltpu.VMEM((chunk, d), BF16),
                            pltpu.VMEM((slots, d), BF16),
                            pltpu.VMEM((slots, LANES), F32),
                            pltpu.VMEM((slots, d), F32),
                            pltpu.VMEM((8, chunk), F32),
                            pltpu.VMEM((chunk, LANES), F32),
                            pltpu.VMEM((2, chunk, chunk), BF16)]),
        out_shape=jax.ShapeDtypeStruct((m, d), F32),
        compiler_params=_params("arbitrary", "arbitrary"),
        name="moe_grouped",
    )(counts, x, gates_t, gext, wg_bf, wu_bf, wd_bf, vec(g), vec(beta))


def _sb_attend(zs, values, u_neg, mask, overlapped=None):
    n = len(zs)
    first_half, out = [], []
    for t in range(n + 1):
        if t < n:
            z = zs[t]() if callable(zs[t]) else zs[t]
            neg_abs = pltpu.bitcast(pltpu.bitcast(z, jnp.uint32) | jnp.uint32(0x80000000), F32)
            soft = jnp.log(1.0 + jnp.exp2(neg_abs)) * LOG2E
            lb = jnp.minimum(z, 0.0) - soft
            nlk = z - lb
            if mask is not None:
                nlk = jnp.where(mask, nlk, 0.0)
            nlk = nlk.astype(BF16)
            first_half.append((nlk, lb, _dot(nlk, u_neg)))
        if t >= 1:
            nlk, lb, later = first_half[t - 1]
            w = jnp.exp2(lb + later)
            if mask is not None:
                w = jnp.where(mask, w, 0.0)
            out.append((values[t - 1](w.astype(BF16)), later[:, 0:1] - nlk[:, 0:1].astype(F32)))
            if overlapped is not None:
                overlapped[t - 1]()
    return out


def _u_neg(size):
    r = lax.broadcasted_iota(jnp.int32, (size, size), 0)
    c = lax.broadcasted_iota(jnp.int32, (size, size), 1)
    return jnp.where(r > c, -1.0, 0.0).astype(BF16)


def _sb_prompt_kernel(bias_ref, q_ref, kt_ref, vt_ref, o_ref, acc_ref, s_ref, z_ref,
                      *, blk, head_dim):
    hp = pl.program_id(0)
    i = pl.program_id(1)
    heads_per = q_ref.shape[1] // head_dim
    q2 = q_ref[...]
    head_of_lane = lax.broadcasted_iota(jnp.int32, (1, q2.shape[1]), 1) // head_dim
    u_neg = _u_neg(blk)
    part_of_lane = lax.broadcasted_iota(jnp.int32, (blk, SB_BIAS_ROWS), 1)
    qs = []
    for h in range(heads_per):
        bias_lanes = jnp.zeros((blk, SB_BIAS_ROWS), F32)
        for part in range(SB_BIAS_PARTS):
            bias_lanes = jnp.where(part_of_lane == part, bias_ref[hp * heads_per + h, part],
                                   bias_lanes)
        qs.append(jnp.concatenate([jnp.where(head_of_lane == h, q2, jnp.zeros_like(q2)),
                                   bias_lanes.astype(BF16)], axis=1))

    def units_of(key_blocks):
        offs = [pl.multiple_of(kb * blk, blk) for kb in key_blocks]
        return [(o, h) for o in offs for h in range(heads_per)]

    def logit_thunks(key_blocks):
        return [functools.partial(lambda o, h: _dot(qs[h], kt_ref[:, pl.ds(o, blk)]), o, h)
                for o, h in units_of(key_blocks)]

    def visit(zs, key_blocks, mask, first, overlapped=None):
        units = units_of(key_blocks)
        values = [functools.partial(lambda o, w: _dot_nt(w, vt_ref[:, pl.ds(o, blk)]), o)
                  for o, _ in units]
        results = _sb_attend(zs, values, u_neg, mask, overlapped)
        for (pv, total), (o, h) in zip(results, units):
            if first:
                acc_ref[h] = pv
                s_ref[h] = total
            else:
                acc_ref[h] += jnp.exp2(s_ref[h]) * pv
                s_ref[h] += total

    r = lax.broadcasted_iota(jnp.int32, (blk, blk), 0)
    c = lax.broadcasted_iota(jnp.int32, (blk, blk), 1)
    visit(logit_thunks([i]), [i], c < r, True)

    done = 0
    run = 1
    while run < SB_KEY_BLOCKS_PER_STEP:
        top = i - done

        @pl.when(i & run != 0)
        def _(top=top, run=run):
            blocks = [top - 1 - j for j in range(run)]
            visit(logit_thunks(blocks), blocks, None, False)

        done = done + (i & run)
        run *= 2

    n_steps = i // SB_KEY_BLOCKS_PER_STEP
    n_units = SB_KEY_BLOCKS_PER_STEP * heads_per

    def step_blocks(step):
        top = i - i % SB_KEY_BLOCKS_PER_STEP - SB_KEY_BLOCKS_PER_STEP * step
        return [jnp.maximum(top - 1 - j, 0) for j in range(SB_KEY_BLOCKS_PER_STEP)]

    def store_logits(u, thunk):
        z_ref[u] = thunk()

    @pl.when(n_steps > 0)
    def _():
        for u, thunk in enumerate(logit_thunks(step_blocks(0))):
            store_logits(u, thunk)

    def body(step, carry):
        refill = [functools.partial(store_logits, u, thunk)
                  for u, thunk in enumerate(logit_thunks(step_blocks(step + 1)))]
        current = [functools.partial(lambda u: z_ref[u], u) for u in range(n_units)]
        visit(current, step_blocks(step), None, False, refill)
        return carry

    lax.fori_loop(0, n_steps, body, 0)
    out = acc_ref[0]
    for h in range(1, heads_per):
        out = jnp.where(head_of_lane == h, acc_ref[h], out)
    o_ref[...] = out


def _sb_prompt(q_bf, kt_bf, vt_bf, bias2, head_dim):
    n, width = q_bf.shape
    n_groups = width // LANES
    blk = _tile(n, 256)
    parts, rest = [], bias2
    for _ in range(SB_BIAS_PARTS):
        part = rest.astype(BF16).astype(F32)
        parts.append(part)
        rest = rest - part
    bias_parts = jnp.stack(parts, axis=1)
    return pl.pallas_call(
        functools.partial(_sb_prompt_kernel, blk=blk, head_dim=head_dim),
        grid=(n_groups, n // blk),
        in_specs=[pl.BlockSpec(memory_space=pltpu.SMEM),
                  pl.BlockSpec((blk, LANES), lambda p, i: (i, p)),
                  pl.BlockSpec((LANES + SB_BIAS_ROWS, n), lambda p, i: (p, 0)),
                  pl.BlockSpec((LANES, n), lambda p, i: (p, 0))],
        out_specs=pl.BlockSpec((blk, LANES), lambda p, i: (i, p)),
        out_shape=jax.ShapeDtypeStruct((n, width), F32),
        scratch_shapes=[pltpu.VMEM((LANES // head_dim, blk, LANES), F32),
                        pltpu.VMEM((LANES // head_dim, blk, 1), F32),
                        pltpu.VMEM((SB_KEY_BLOCKS_PER_STEP * (LANES // head_dim), blk, blk), F32)],
        compiler_params=_params("parallel", "arbitrary"),
        name="sb_prompt",
    )(bias_parts, q_bf, kt_bf, vt_bf)


def _sb_sample_kernel(pt_ref, qbd_ref, knew_ref, vnew_ref, bias_ref, kpool_ref, vpool_ref,
                      o_ref, kbuf, vbuf, sem, pad_k, pad_v, acc_ref,
                      *, n_batch, n_pages, page, group, sub, t_new):
    b = pl.program_id(0)
    n_steps = n_pages // group
    total_steps = n_batch * n_steps
    ahead = SB_PAGE_SLOTS - 1

    def copies(gstep):
        seq = gstep // n_steps
        first = (n_steps - 1 - gstep % n_steps) * group
        slot = gstep % SB_PAGE_SLOTS
        out = []
        for r in range(group):
            phys = pt_ref[seq, first + r]
            out.append(pltpu.make_async_copy(kpool_ref.at[phys], kbuf.at[slot, r], sem.at[0, slot]))
            out.append(pltpu.make_async_copy(vpool_ref.at[phys], vbuf.at[slot, r], sem.at[1, slot]))
        return out

    @pl.when(b == 0)
    def _():
        for gstep in range(min(ahead, total_steps)):
            for cp in copies(gstep):
                cp.start()

    qbd = qbd_ref[0]
    bias = bias_ref[...]
    rows = qbd.shape[0]

    pad_k[...] = jnp.zeros(pad_k.shape, F32)
    pad_v[...] = jnp.zeros(pad_v.shape, F32)
    pad_k[0:t_new, :] = knew_ref[0]
    pad_v[0:t_new, :] = vnew_ref[0]
    t_of_row = lax.broadcasted_iota(jnp.int32, (rows, page), 0) % SB_ROWS_PER_HEAD
    s_of_col = lax.broadcasted_iota(jnp.int32, (rows, page), 1)
    (pv, s_after), = _sb_attend([_dot_nt(qbd, pad_k[...].astype(BF16)) + bias],
                                [lambda w: _dot(w, pad_v[...].astype(BF16))], _u_neg(page),
                                s_of_col < t_of_row)
    acc_ref[...] = pv

    per_sub = sub // page
    u_neg = _u_neg(sub)

    def body(step, s_after):
        gstep = b * n_steps + step
        slot = gstep % SB_PAGE_SLOTS
        for cp in copies(gstep):
            cp.wait()

        @pl.when(gstep + ahead < total_steps)
        def _():
            for cp in copies(gstep + ahead):
                cp.start()

        def keys_of(buf, first):
            return jnp.concatenate([buf[slot, first + j].astype(BF16) for j in range(per_sub)],
                                   axis=1)

        firsts = list(reversed(range(0, group, per_sub)))
        zs = [_dot(qbd, keys_of(kbuf, f)) + bias for f in firsts]
        values = [functools.partial(lambda f, w: _dot_nt(w, keys_of(vbuf, f)), f) for f in firsts]
        upd = jnp.zeros(acc_ref.shape, F32)
        for pv, total in _sb_attend(zs, values, u_neg, None):
            upd = upd + jnp.exp2(s_after) * pv
            s_after = s_after + total
        acc_ref[...] += upd
        return s_after

    lax.fori_loop(0, n_steps, body, s_after)

    width = acc_ref.shape[1]
    n_heads = rows // SB_ROWS_PER_HEAD
    head_dim = width // n_heads
    head_of_row = lax.broadcasted_iota(jnp.int32, (rows, width), 0) // SB_ROWS_PER_HEAD
    head_of_col = lax.broadcasted_iota(jnp.int32, (rows, width), 1) // head_dim
    own = jnp.where(head_of_row == head_of_col, acc_ref[...], 0.0)
    out = own[0:SB_ROWS_PER_HEAD, :]
    for h in range(1, n_heads):
        out = out + own[h * SB_ROWS_PER_HEAD:(h + 1) * SB_ROWS_PER_HEAD, :]
    o_ref[0] = out[0:t_new, :]


def _sb_sample(q, k_new, v_new, bias2, pool_kt, pool_vt, page_table, q_scale):
    b, t_new, width = q.shape
    n_heads = bias2.shape[0]
    head_dim = width // n_heads
    page = pool_kt.shape[2]
    n_pages = page_table.shape[1]
    group = max(g for g in (4, 2, 1) if n_pages % g == 0)
    sub = min(2, group) * page
    rows = n_heads * SB_ROWS_PER_HEAD
    head_of_col = jnp.arange(width) // head_dim
    qpad = jnp.pad(q * q_scale, ((0, 0), (0, SB_ROWS_PER_HEAD - t_new), (0, 0)))
    qbd = jnp.where(head_of_col[None, None, None, :] == jnp.arange(n_heads)[None, :, None, None],
                    qpad[:, None, :, :], 0.0).reshape(b, rows, width).astype(BF16)
    bias_rows = jnp.repeat(bias2, SB_ROWS_PER_HEAD).reshape(rows, 1)
    return pl.pallas_call(
        functools.partial(_sb_sample_kernel, n_batch=b, n_pages=n_pages, page=page, group=group,
                          sub=sub, t_new=t_new),
        grid_spec=pltpu.PrefetchScalarGridSpec(
            num_scalar_prefetch=1,
            grid=(b,),
            in_specs=[pl.BlockSpec((1, rows, width), lambda i, pt: (i, 0, 0)),
                      pl.BlockSpec((1, t_new, width), lambda i, pt: (i, 0, 0)),
                      pl.BlockSpec((1, t_new, width), lambda i, pt: (i, 0, 0)),
                      pl.BlockSpec((rows, 1), lambda i, pt: (0, 0)),
                      pl.BlockSpec(memory_space=pl.ANY),
                      pl.BlockSpec(memory_space=pl.ANY)],
            out_specs=pl.BlockSpec((1, t_new, width), lambda i, pt: (i, 0, 0)),
            scratch_shapes=[pltpu.VMEM((SB_PAGE_SLOTS, group, width, page), F32),
                            pltpu.VMEM((SB_PAGE_SLOTS, group, width, page), F32),
                            pltpu.SemaphoreType.DMA((2, SB_PAGE_SLOTS)),
                            pltpu.VMEM((page, width), F32),
                            pltpu.VMEM((page, width), F32),
                            pltpu.VMEM((rows, width), F32)]),
        out_shape=jax.ShapeDtypeStruct((b, t_new, width), F32),
        compiler_params=_params("arbitrary"),
        name="sb_sample",
    )(page_table, qbd, k_new, v_new, bias_rows, pool_kt, pool_vt)


def _feature_major(t):
    lead = t.shape[:-3]
    tokens, heads, dim = t.shape[-3:]
    n = len(lead)
    return jnp.transpose(t, (*range(n), n + 1, n + 2, n)).reshape(*lead, heads * dim, tokens)


def _token_major(t, heads):
    lead = t.shape[:-2]
    width, tokens = t.shape[-2:]
    n = len(lead)
    t = t.reshape(*lead, heads, width // heads, tokens)
    return jnp.transpose(t, (*range(n), n + 2, n, n + 1))


def kernel(x_prompt, x_sample, state_conv, cache_sb_k, cache_sb_v, cache_mem_k, cache_mem_v, page_table, mem_prompt, a_w_in, a_b_in, a_conv_w, a_conv_b, a_norm_g, a_norm_b, a_w_out, a_b_out, b_w_in, b_b_in, b_sb_bias, b_w_out, b_b_out, w_mem_kv, ln_mix_g, ln_mix_b, ln_ffn_g, ln_ffn_b, w_router, b_router, w_exp_gate, w_exp_up, w_exp_down):
    depth = w_mem_kv.shape[0]
    n_mixers = 2
    alpha = (2.0 * depth) ** 0.25
    bp, seq, d = x_prompt.shape
    bs, t_new, _ = x_sample.shape
    mem_heads, mem_hd = cache_mem_k.shape[-2:]
    mem_w = mem_heads * mem_hd
    sb_heads, sb_hd = cache_sb_k.shape[-2:]
    sb_w = sb_heads * sb_hd
    conv_ch = a_conv_w.shape[-1]
    mem_tokens = mem_prompt.shape[1]
    assert bp == 1
    sb_q_scale = float(sb_hd) ** -0.5 * LOG2E

    y_p = x_prompt.reshape(seq, d)
    y_s = x_sample.reshape(bs * t_new, d)
    mem_p = mem_prompt.reshape(mem_tokens, d)
    conv_p, conv_s, sbk_p, sbv_p, sbk_s, sbv_s, memk_p, memv_p = ([] for _ in range(8))

    for layer in range(depth):
        kvt = _linear_t(mem_p, w_mem_kv[layer].T.astype(BF16), name="mem_kv")
        mkt_p = kvt[:mem_w][None]
        mvt_p = kvt[mem_w:][None]
        memk_p.append(_token_major(mkt_p, mem_heads))
        memv_p.append(_token_major(mvt_p, mem_heads))
        mkt_s = _feature_major(cache_mem_k[layer])
        mvt_s = _feature_major(cache_mem_v[layer])
        i = layer // n_mixers
        if layer % n_mixers == 0:
            w_in = a_w_in[i].astype(BF16)
            proj_p = _linear(y_p, w_in, a_b_in[i], name="conv_in_proj")
            proj_s = _linear(y_s, w_in, a_b_in[i], name="conv_in_proj_s")
            mix_p, st_p = _conv_prompt(proj_p, a_conv_w[i], a_conv_b[i], a_norm_g[i], a_norm_b[i])
            conv_p.append(st_p[None])
            proj_tm = jnp.transpose(proj_s.reshape(bs, t_new, -1), (1, 0, 2))
            mix_s, st_s = _conv_sample(proj_tm, jnp.transpose(state_conv[i], (1, 0, 2)),
                                       a_conv_w[i], a_conv_b[i], a_norm_g[i], a_norm_b[i])
            mix_s = jnp.transpose(mix_s, (1, 0, 2)).reshape(bs * t_new, conv_ch)
            conv_s.append(jnp.transpose(st_s, (1, 0, 2)))
            q_col = (2 * conv_ch) // mem_w
            assert q_col * mem_w == 2 * conv_ch
            qm_p, qm_s = proj_p[None], proj_s.reshape(bs, t_new, -1)
            w_out, b_out = a_w_out[i], a_b_out[i]
        else:
            bias2 = b_sb_bias[i] * LOG2E
            q_bf, kt, vt, kt_bf, vt_bf, qm = _sb_in_proj(y_p, b_w_in[i], b_b_in[i], sb_w, sb_q_scale)
            sbk_p.append(_token_major(kt[None], sb_heads))
            sbv_p.append(_token_major(vt[None], sb_heads))
            mix_p = _sb_prompt(q_bf, kt_bf, vt_bf, bias2, sb_hd)
            proj_s = _linear(y_s, b_w_in[i].astype(BF16), b_b_in[i], name="sb_in_proj_s")
            proj_s3 = proj_s.reshape(bs, t_new, -1)
            k_s = proj_s3[:, :, sb_w:2 * sb_w]
            v_s = proj_s3[:, :, 2 * sb_w:3 * sb_w]
            sbk_s.append(k_s.reshape(bs, t_new, sb_heads, sb_hd))
            sbv_s.append(v_s.reshape(bs, t_new, sb_heads, sb_hd))
            mix_s = _sb_sample(proj_s3[:, :, :sb_w], k_s, v_s, bias2,
                               _feature_major(cache_sb_k[i]), _feature_major(cache_sb_v[i]),
                               page_table, sb_q_scale)
            mix_s = mix_s.reshape(bs * t_new, sb_w)
            q_col = (3 * sb_w) // mem_w
            assert q_col * mem_w == 3 * sb_w
            qm_p, qm_s = qm[None], proj_s3
            q_col_p = 0
            w_out, b_out = b_w_out[i], b_b_out[i]

        mem_out_p = _mem_attn(qm_p, q_col if layer % n_mixers == 0 else q_col_p, mkt_p, mvt_p,
                              mem_heads)
        mem_out_s = _mem_attn(qm_s, q_col, mkt_s, mvt_s, mem_heads)
        w_out_bf = w_out.astype(BF16)
        y_p = _out_proj(y_p, mix_p, mem_out_p.reshape(seq, mem_w), w_out_bf, b_out,
                        ln_mix_g[layer], ln_mix_b[layer], alpha)
        y_s = _out_proj(y_s, mix_s, mem_out_s.reshape(bs * t_new, mem_w), w_out_bf, b_out,
                        ln_mix_g[layer], ln_mix_b[layer], alpha)

        wg, wu, wd = (w[layer].astype(BF16) for w in (w_exp_gate, w_exp_up, w_exp_down))
        y_p = _moe_grouped(y_p, _router(y_p, w_router, b_router), wg, wu, wd,
                           ln_ffn_g[layer], ln_ffn_b[layer], alpha)
        y_s = _moe(y_s, _router(y_s, w_router, b_router).T, wg, wu, wd,
                   ln_ffn_g[layer], ln_ffn_b[layer], alpha)

    return (y_p.reshape(bp, seq, d), y_s.reshape(bs, t_new, d),
            jnp.stack(conv_p), jnp.stack(conv_s), jnp.stack(sbk_p), jnp.stack(sbv_p),
            jnp.stack(sbk_s), jnp.stack(sbv_s), jnp.stack(memk_p), jnp.stack(memv_p))
```

```python
import functools
import math

import jax
import jax.numpy as jnp
from jax import lax
from jax.experimental import pallas as pl
from jax.experimental.pallas import tpu as pltpu

LN_EPS = 1e-5
N_GROUPS = 4
TOP_K = 2
CONV_HALO = 32
LANES = 128
MATMUL_ROWS = 512
SB_ROWS_PER_HEAD = 8
SB_KEY_BLOCKS_PER_STEP = 16
SB_PAGE_SLOTS = 3
SB_BIAS_ROWS = 16
SB_BIAS_PARTS = 3
MOE_CHUNK = 1024
MOE_EXPERTS_PER_STEP = 4
VMEM_LIMIT = 48 * 1024 * 1024
MOE_VMEM_LIMIT = 60 * 1024 * 1024
LOG2E = math.log2(math.e)

BF16 = jnp.bfloat16
F32 = jnp.float32


def _tile(n, pref):
    if n <= pref:
        return n
    t = pref - pref % 8
    while t >= 8:
        if n % t == 0:
            return t
        t -= 8
    return n


def _params(*sem, vmem_limit=VMEM_LIMIT):
    return pltpu.CompilerParams(dimension_semantics=sem, vmem_limit_bytes=vmem_limit)


def _layer_norm(x, g, b):
    mu = jnp.mean(x, axis=-1, keepdims=True)
    xc = x - mu
    var = jnp.mean(xc * xc, axis=-1, keepdims=True)
    return xc * lax.rsqrt(var + LN_EPS) * g + b


def _dot(a, b):
    return jnp.dot(a, b, preferred_element_type=F32)


def _dot_nt(a, b):
    return lax.dot_general(a, b, (((1,), (1,)), ((), ())), preferred_element_type=F32)


def _linear_kernel(x_ref, w_ref, b_ref, o_ref):
    o_ref[...] = _dot(x_ref[...].astype(BF16), w_ref[...]) + b_ref[...]


def _linear(x, w_bf, b, *, name="linear"):
    m, k = x.shape
    n = w_bf.shape[1]
    tm = _tile(m, MATMUL_ROWS)
    return pl.pallas_call(
        _linear_kernel,
        grid=(m // tm,),
        in_specs=[pl.BlockSpec((tm, k), lambda i: (i, 0)),
                  pl.BlockSpec((k, n), lambda i: (0, 0)),
                  pl.BlockSpec((1, n), lambda i: (0, 0))],
        out_specs=pl.BlockSpec((tm, n), lambda i: (i, 0)),
        out_shape=jax.ShapeDtypeStruct((m, n), F32),
        compiler_params=_params("parallel"),
        name=name,
    )(x, w_bf, b.reshape(1, n))


def _linear_t_kernel(x_ref, wt_ref, o_ref):
    o_ref[...] = _dot_nt(wt_ref[...], x_ref[...].astype(BF16))


def _linear_t(x, wt_bf, *, name="linear_t"):
    m, k = x.shape
    n = wt_bf.shape[0]
    tm = _tile(m, 256)
    return pl.pallas_call(
        _linear_t_kernel,
        grid=(m // tm,),
        in_specs=[pl.BlockSpec((tm, k), lambda i: (i, 0)),
                  pl.BlockSpec((n, k), lambda i: (0, 0))],
        out_specs=pl.BlockSpec((n, tm), lambda i: (0, i)),
        out_shape=jax.ShapeDtypeStruct((n, m), F32),
        compiler_params=_params("parallel"),
        name=name,
    )(x, wt_bf)


def _sb_in_proj_kernel(x_ref, wq_ref, wkt_ref, wvt_ref, wm_ref, bq_ref, bk_ref, bv_ref, bm_ref,
                       q_ref, kt_ref, vt_ref, ktb_ref, vtb_ref, qm_ref, *, q_scale):
    xb = x_ref[...].astype(BF16)
    q_ref[...] = ((_dot(xb, wq_ref[...]) + bq_ref[...]) * q_scale).astype(BF16)
    kt = _dot_nt(wkt_ref[...], xb) + bk_ref[...]
    kt_ref[...] = kt
    ones = jnp.ones((SB_BIAS_ROWS, kt.shape[1]), BF16)
    for p in range(kt.shape[0] // LANES):
        base = p * (LANES + SB_BIAS_ROWS)
        ktb_ref[base:base + LANES, :] = kt[p * LANES:(p + 1) * LANES].astype(BF16)
        ktb_ref[base + LANES:base + LANES + SB_BIAS_ROWS, :] = ones
    vt = _dot_nt(wvt_ref[...], xb) + bv_ref[...]
    vt_ref[...] = vt
    vtb_ref[...] = vt.astype(BF16)
    qm_ref[...] = _dot(xb, wm_ref[...]) + bm_ref[...]


def _sb_in_proj(x, w_in, b_in, sb_w, q_scale):
    n, d = x.shape
    wm = w_in.shape[1] - 3 * sb_w
    tm = _tile(n, MATMUL_ROWS)
    kt_rows =sb_w // LANES * (LANES + SB_BIAS_ROWS)
    wq = w_in[:, :sb_w].astype(BF16)
    wkt = w_in[:, sb_w:2 * sb_w].T.astype(BF16)
    wvt = w_in[:, 2 * sb_w:3 * sb_w].T.astype(BF16)
    wmem = w_in[:, 3 * sb_w:].astype(BF16)
    const = lambda shape: pl.BlockSpec(shape, lambda i: (0, 0))
    return pl.pallas_call(
        functools.partial(_sb_in_proj_kernel, q_scale=q_scale),
        grid=(n // tm,),
        in_specs=[pl.BlockSpec((tm, d), lambda i: (i, 0)),
                  const((d, sb_w)), const((sb_w, d)), const((sb_w, d)), const((d, wm)),
                  const((1, sb_w)), const((sb_w, 1)), const((sb_w, 1)), const((1, wm))],
        out_specs=[pl.BlockSpec((tm, sb_w), lambda i: (i, 0)),
                   pl.BlockSpec((sb_w, tm), lambda i: (0, i)),
                   pl.BlockSpec((sb_w, tm), lambda i: (0, i)),
                   pl.BlockSpec((kt_rows, tm), lambda i: (0, i)),
                   pl.BlockSpec((sb_w, tm), lambda i: (0, i)),
                   pl.BlockSpec((tm, wm), lambda i: (i, 0))],
        out_shape=[jax.ShapeDtypeStruct((n, sb_w), BF16),
                   jax.ShapeDtypeStruct((sb_w, n), F32),
                   jax.ShapeDtypeStruct((sb_w, n), F32),
                   jax.ShapeDtypeStruct((kt_rows, n), BF16),
                   jax.ShapeDtypeStruct((sb_w, n), BF16),
                   jax.ShapeDtypeStruct((n, wm), F32)],
        compiler_params=_params("parallel"),
        name="sb_in_proj",
    )(x, wq, wkt, wvt, wmem,
      b_in[:sb_w].reshape(1, sb_w), b_in[sb_w:2 * sb_w].reshape(sb_w, 1),
      b_in[2 * sb_w:3 * sb_w].reshape(sb_w, 1), b_in[3 * sb_w:].reshape(1, wm))


def _conv_prompt_kernel(a_ref, gate_ref, w_ref, cb_ref, ng_ref, nb_ref,
                        mix_ref, st_ref, f_ref, g_ref, *, tile, width, chunk):
    i = pl.program_id(0)
    halo = CONV_HALO
    sub = 8

    @pl.when(i == 0)
    def _():
        f_ref[0:halo, :] = jnp.zeros((halo, f_ref.shape[1]), F32)

    f_ref[halo:halo + tile, :] = a_ref[...] * jax.nn.sigmoid(gate_ref[...])
    shifted_rows = g_ref.shape[1]
    for b in range(1, sub):
        g_ref[b - 1] = f_ref[b:b + shifted_rows, :]
    first = halo - (width - 1)
    for c in range(tile // chunk):
        y = jnp.zeros((chunk, f_ref.shape[1]), F32) + cb_ref[...]
        for k in range(width):
            b = (first + k) % sub
            row = c * chunk + (first + k) - b
            src = f_ref[row:row + chunk, :] if b == 0 else g_ref[b - 1, row:row + chunk, :]
            y = y + src * w_ref[k:k + 1, :]
        y = _layer_norm(y, ng_ref[...], nb_ref[...])
        mix_ref[c * chunk:(c + 1) * chunk, :] = y * jax.nn.sigmoid(y)

    @pl.when(i == pl.num_programs(0) - 1)
    def _():
        st_ref[...] = f_ref[halo + tile - (width - 1):halo + tile, :]

    f_ref[0:halo, :] = f_ref[tile:tile + halo, :]


def _conv_prompt(proj, conv_w, conv_b, norm_g, norm_b):
    n = proj.shape[0]
    width, c = conv_w.shape
    tile = _tile(n, 256)
    chunk = _tile(tile, 32)
    vec = lambda v: v.reshape(1, c)
    row = pl.BlockSpec((1, c), lambda i: (0, 0))
    return pl.pallas_call(
        functools.partial(_conv_prompt_kernel, tile=tile, width=width, chunk=chunk),
        grid=(n // tile,),
        in_specs=[pl.BlockSpec((tile, c), lambda i: (i, 0)),
                  pl.BlockSpec((tile, c), lambda i: (i, 1)),
                  pl.BlockSpec((width, c), lambda i: (0, 0)),
                  row, row, row],
        out_specs=[pl.BlockSpec((tile, c), lambda i: (i, 0)),
                   pl.BlockSpec((width - 1, c), lambda i: (0, 0))],
        out_shape=[jax.ShapeDtypeStruct((n, c), F32),
                   jax.ShapeDtypeStruct((width - 1, c), F32)],
        scratch_shapes=[pltpu.VMEM((tile + CONV_HALO, c), F32),
                        pltpu.VMEM((7, tile + CONV_HALO - 8, c), F32)],
        compiler_params=_params("arbitrary"),
        name="conv_prompt",
    )(proj, proj, conv_w, vec(conv_b), vec(norm_g), vec(norm_b))


def _conv_sample_kernel(a_ref, gate_ref, past_ref, w_ref, cb_ref, ng_ref, nb_ref,
                        mix_ref, st_ref, *, t_new, width):
    hist = width - 1
    new = [a_ref[t] * jax.nn.sigmoid(gate_ref[t]) for t in range(t_new)]
    full = lambda j: past_ref[j] if j < hist else new[j - hist]
    for t in range(t_new):
        y = jnp.zeros(new[0].shape, F32) + cb_ref[...]
        for k in range(width):
            y = y + full(t + k) * w_ref[k:k + 1, :]
        y = _layer_norm(y, ng_ref[...], nb_ref[...])
        mix_ref[t] = y * jax.nn.sigmoid(y)
    for j in range(hist):
        st_ref[j] = full(j + t_new)


def _conv_sample(proj_tm, past_tm, conv_w, conv_b, norm_g, norm_b):
    t_new, batch, _ = proj_tm.shape
    width, c = conv_w.shape
    vec = lambda v: v.reshape(1, c)
    row = pl.BlockSpec((1, c), lambda i: (0, 0))
    return pl.pallas_call(
        functools.partial(_conv_sample_kernel, t_new=t_new, width=width),
        grid=(1,),
        in_specs=[pl.BlockSpec((t_new, batch, c), lambda i: (0, 0, 0)),
                  pl.BlockSpec((t_new, batch, c), lambda i: (0, 0, 1)),
                  pl.BlockSpec((width - 1, batch, c), lambda i: (0, 0, 0)),
                  pl.BlockSpec((width, c), lambda i: (0, 0)),
                  row, row, row],
        out_specs=[pl.BlockSpec((t_new, batch, c), lambda i: (0, 0, 0)),
                   pl.BlockSpec((width - 1, batch, c), lambda i: (0, 0, 0))],
        out_shape=[jax.ShapeDtypeStruct((t_new, batch, c), F32),
                   jax.ShapeDtypeStruct((width - 1, batch, c), F32)],
        compiler_params=_params("arbitrary"),
        name="conv_sample",
    )(proj_tm, proj_tm, past_tm, conv_w, vec(conv_b), vec(norm_g), vec(norm_b))


def _mem_attn_kernel(q_ref, kt_ref, vt_ref, o_ref, *, heads, scale):
    q = q_ref[0]
    kt = kt_ref[0].astype(BF16)
    vt = vt_ref[0].astype(BF16)
    width = q.shape[-1]
    head_of_lane = lax.broadcasted_iota(jnp.int32, (1, width), 1) // (width // heads)
    out = jnp.zeros(q.shape, F32)
    for h in range(heads):
        sel = head_of_lane == h
        s = _dot(jnp.where(sel, q, 0.0).astype(BF16), kt) * scale
        p = jnp.exp(s - jnp.max(s, axis=-1, keepdims=True))
        p = p / jnp.sum(p, axis=-1, keepdims=True)
        out = out + jnp.where(sel, _dot_nt(p.astype(BF16), vt), 0.0)
    o_ref[0] = out


def _mem_attn(q_src, col_block, mem_kt, mem_vt, heads):
    b, t, _ = q_src.shape
    width, m = mem_kt.shape[1:]
    tq = _tile(t, 512)
    scale = float(width // heads) ** -0.5
    return pl.pallas_call(
        functools.partial(_mem_attn_kernel, heads=heads, scale=scale),
        grid=(b, t // tq),
        in_specs=[pl.BlockSpec((1, tq, width), lambda i, j: (i, j, col_block)),
                  pl.BlockSpec((1, width, m), lambda i, j: (i, 0, 0)),
                  pl.BlockSpec((1, width, m), lambda i, j: (i, 0, 0))],
        out_specs=pl.BlockSpec((1, tq, width), lambda i, j: (i, j, 0)),
        out_shape=jax.ShapeDtypeStruct((b, t, width), F32),
        compiler_params=_params("parallel", "parallel"),
        name="mem_attn",
    )(q_src, mem_kt, mem_vt)


def _out_proj_kernel(x_ref, mix_ref, mem_ref, wa_ref, wb_ref, b_ref, g_ref, beta_ref,
                     o_ref, *, alpha):
    acc = _dot(mix_ref[...].astype(BF16), wa_ref[...])
    acc = acc + _dot(mem_ref[...].astype(BF16), wb_ref[...]) + b_ref[...]
    o_ref[...] = _layer_norm(alpha * x_ref[...] + acc, g_ref[...], beta_ref[...])


def _out_proj(x, mix, mem, w_out_bf, b_out, g, beta, alpha):
    m, d = x.shape
    cm = mix.shape[1]
    cw = mem.shape[1]
    tm = _tile(m, MATMUL_ROWS)
    vec = lambda v: v.reshape(1, d)
    row = pl.BlockSpec((1, d), lambda i: (0, 0))
    return pl.pallas_call(
        functools.partial(_out_proj_kernel, alpha=alpha),
        grid=(m // tm,),
        in_specs=[pl.BlockSpec((tm, d), lambda i: (i, 0)),
                  pl.BlockSpec((tm, cm), lambda i: (i, 0)),
                  pl.BlockSpec((tm, cw), lambda i: (i, 0)),
                  pl.BlockSpec((cm, d), lambda i: (0, 0)),
                  pl.BlockSpec((cw, d), lambda i: (0, 0)),
                  row, row, row],
        out_specs=pl.BlockSpec((tm, d), lambda i: (i, 0)),
        out_shape=jax.ShapeDtypeStruct((m, d), F32),
        compiler_params=_params("parallel"),
        name="out_proj_ln",
    )(x, mix, mem, w_out_bf[:cm], w_out_bf[cm:], vec(b_out), vec(g), vec(beta))


def _top2(rows):
    best, bi = rows[0], jnp.zeros(rows[0].shape, jnp.int32)
    for j in range(1, len(rows)):
        upd = rows[j] > best
        best = jnp.where(upd, rows[j], best)
        bi = jnp.where(upd, j, bi)
    sec, si = None, None
    for j in range(len(rows)):
        cand = jnp.where(bi == j, -jnp.inf, rows[j])
        if sec is None:
            sec, si = cand, jnp.zeros(cand.shape, jnp.int32)
        else:
            upd = cand > sec
            sec = jnp.where(upd, cand, sec)
            si = jnp.where(upd, j, si)
    return best, sec, bi, si


def _router_kernel(x_ref, wt_ref, b_ref, o_ref, *, n_groups):
    logits = lax.dot_general(wt_ref[...], x_ref[...], (((1,), (1,)), ((), ())),
                             preferred_element_type=F32,
                             precision=lax.Precision.HIGHEST) + b_ref[...]
    n_exp = logits.shape[0]
    per = n_exp // n_groups
    e = jnp.exp(logits - jnp.max(logits, axis=0, keepdims=True))
    probs = e / jnp.sum(e, axis=0, keepdims=True)
    picks = [_top2([probs[g * per + j:g * per + j + 1, :] for j in range(per)])
             for g in range(n_groups)]
    score = picks[0][0] + picks[0][1]
    sel = picks[0]
    g_sel = jnp.zeros(score.shape, jnp.int32)
    for g in range(1, n_groups):
        sc = picks[g][0] + picks[g][1]
        upd = sc > score
        score = jnp.where(upd, sc, score)
        g_sel = jnp.where(upd, g, g_sel)
        sel = tuple(jnp.where(upd, a, b) for a, b in zip(picks[g], sel))
    p1, p2, i1, i2 = sel
    denom = p1 + p2
    e1 = g_sel * per + i1
    e2 = g_sel * per + i2
    eidx = lax.broadcasted_iota(jnp.int32, logits.shape, 0)
    o_ref[...] = (jnp.where(eidx == e1, p1 / denom, 0.0)
                  + jnp.where(eidx == e2, p2 / denom, 0.0))


def _router(x, w_router, b_router):
    m, d = x.shape
    n_exp = w_router.shape[1]
    tm = _tile(m, 512)
    return pl.pallas_call(
        functools.partial(_router_kernel, n_groups=N_GROUPS),
        grid=(m // tm,),
        in_specs=[pl.BlockSpec((tm, d), lambda i: (i, 0)),
                  pl.BlockSpec((n_exp, d), lambda i: (0, 0)),
                  pl.BlockSpec((n_exp, 1), lambda i: (0, 0))],
        out_specs=pl.BlockSpec((n_exp, tm), lambda i: (0, i)),
        out_shape=jax.ShapeDtypeStruct((n_exp, m), F32),
        compiler_params=_params("parallel"),
        name="router",
    )(x, w_router.T, b_router.reshape(n_exp, 1))


def _moe_kernel(x_ref, gates_ref, wg_ref, wu_ref, wd_ref, g_ref, beta_ref, o_ref,
                xb_ref, acc_ref, *, alpha):
    e = pl.program_id(1)

    @pl.when(e == 0)
    def _():
        xb_ref[...] = x_ref[...].astype(BF16)
        acc_ref[...] = jnp.zeros(acc_ref.shape, F32)

    gates = gates_ref[...]
    lane = lax.broadcasted_iota(jnp.int32, gates.shape, 1)
    gate = jnp.sum(jnp.where(lane == e, gates, 0.0), axis=1, keepdims=True)
    xb = xb_ref[...]
    hg = _dot(xb, wg_ref[0])
    hu = _dot(xb, wu_ref[0])
    h = hg * jax.nn.sigmoid(hg) * hu * gate
    acc_ref[...] += _dot(h.astype(BF16), wd_ref[0])

    @pl.when(e == pl.num_programs(1) - 1)
    def _():
        o_ref[...] = _layer_norm(alpha * x_ref[...] + acc_ref[...], g_ref[...], beta_ref[...])


def _moe(x, gates, wg_bf, wu_bf, wd_bf, g, beta, alpha):
    m, d = x.shape
    n_exp, _, ff = wg_bf.shape
    tm = _tile(m, 512)
    vec = lambda v: v.reshape(1, d)
    row = pl.BlockSpec((1, d), lambda i, e: (0, 0))
    return pl.pallas_call(
        functools.partial(_moe_kernel, alpha=alpha),
        grid=(m // tm, n_exp),
        in_specs=[pl.BlockSpec((tm, d), lambda i, e: (i, 0)),
                  pl.BlockSpec((tm, n_exp), lambda i, e: (i, 0)),
                  pl.BlockSpec((1, d, ff), lambda i, e: (e, 0, 0)),
                  pl.BlockSpec((1, d, ff), lambda i, e: (e, 0, 0)),
                  pl.BlockSpec((1, ff, d), lambda i, e: (e, 0, 0)),
                  row, row],
        out_specs=pl.BlockSpec((tm, d), lambda i, e: (i, 0)),
        out_shape=jax.ShapeDtypeStruct((m, d), F32),
        scratch_shapes=[pltpu.VMEM((tm, d), BF16), pltpu.VMEM((tm, d), F32)],
        compiler_params=_params("parallel", "arbitrary"),
        name="moe",
    )(x, gates, wg_bf, wu_bf, wd_bf, vec(g), vec(beta))


def _moe_grouped_kernel(cnt_ref, x_ref, gt_ref, gext_ref, wg_ref, wu_ref, wd_ref, g_ref, beta_ref,
                        o_ref, xb_ref, xs_ref, gs_ref, ys_ref, rrow_ref, rcol_ref, before_ref,
                        *, alpha, per, tile):
    c = pl.program_id(0)
    step = pl.program_id(1)
    e = step * MOE_EXPERTS_PER_STEP
    g = e // per
    j = e % per
    chunk, d = x_ref.shape
    n_exp = gt_ref.shape[0]
    n_tiles = (cnt_ref[c, g] + tile - 1) // tile

    @pl.when((c == 0) & (e == 0))
    def _():
        r = lax.broadcasted_iota(jnp.int32, (chunk, chunk), 0)
        cc = lax.broadcasted_iota(jnp.int32, (chunk, chunk), 1)
        before_ref[0] = jnp.where(r < cc, 1.0, 0.0).astype(BF16)
        before_ref[1] = jnp.where(cc < r, 1.0, 0.0).astype(BF16)

    @pl.when(e == 0)
    def _():
        xb_ref[...] = x_ref[...].astype(BF16)
        o_ref[...] = jnp.zeros(o_ref.shape, F32)
        gt = gt_ref[...]
        row8 = lax.broadcasted_iota(jnp.int32, (8, chunk), 0)
        member = jnp.zeros((8, chunk), F32)
        for grp in range(n_exp // per):
            in_grp = jnp.sum(gt[grp * per:(grp + 1) * per, :], axis=0, keepdims=True) > 0.0
            member = jnp.where((row8 == grp) & in_grp, 1.0, member)
        rank = _dot(member.astype(BF16), before_ref[0])
        rrow_ref[...] = jnp.where(member > 0.0, rank, -1.0)
        lane_l = lax.broadcasted_iota(jnp.int32, (LANES, LANES), 0)
        lane_g = lax.broadcasted_iota(jnp.int32, (LANES, LANES), 1)
        fold = jnp.where((lane_l < n_exp) & (lane_l // per == lane_g), 1.0, 0.0).astype(BF16)
        picked = jnp.where(gext_ref[...].astype(F32) > 0.0, 1.0, 0.0).astype(BF16)
        member_c = jnp.where(_dot(picked, fold) > 0.0, 1.0, 0.0)
        rank_c = _dot(before_ref[1], member_c.astype(BF16))
        rcol_ref[...] = jnp.where(member_c > 0.0, rank_c, -1.0)

    @pl.when(j == 0)
    def _():
        rank_row = rrow_ref[pl.ds(g, 1), :]

        def gather(t, carry):
            rows = pl.ds(pl.multiple_of(t * tile, 16), tile)
            slot = t * tile + lax.broadcasted_iota(jnp.int32, (tile, 1), 0)
            onehot = jnp.where(rank_row == slot.astype(F32), 1.0, 0.0).astype(BF16)
            xs_ref[rows, :] = _dot(onehot, xb_ref[...]).astype(BF16)
            gs_ref[rows, :] = _dot(onehot, gext_ref[...])
            ys_ref[rows, :] = jnp.zeros((tile, d), F32)
            return carry

        lax.fori_loop(0, n_tiles, gather, 0)

    lane = lax.broadcasted_iota(jnp.int32, (tile, LANES), 1)

    def experts(t, carry):
        rows = pl.ds(pl.multiple_of(t * tile, 16), tile)
        xs = xs_ref[rows, :]
        gs = gs_ref[rows, :]
        out = jnp.zeros((tile, d), F32)
        for k in range(MOE_EXPERTS_PER_STEP):
            gate_lanes = (lane == e + k) | (lane == e + k + n_exp)
            gate = jnp.sum(jnp.where(gate_lanes, gs, 0.0), axis=1, keepdims=True)
            hg = _dot(xs, wg_ref[k])
            hu = _dot(xs, wu_ref[k])
            h = hg * jax.nn.sigmoid(hg) * hu * gate
            out = out + _dot(h.astype(BF16), wd_ref[k])
        ys_ref[rows, :] += out
        return carry

    lax.fori_loop(0, n_tiles, experts, 0)

    @pl.when(j + MOE_EXPERTS_PER_STEP == per)
    def _():
        lane_c = lax.broadcasted_iota(jnp.int32, (chunk, LANES), 1)
        rank_col = jnp.sum(jnp.where(lane_c == g, rcol_ref[...], 0.0), axis=1, keepdims=True)

        def scatter(t, carry):
            rows = pl.ds(pl.multiple_of(t * tile, 16), tile)
            slot = t * tile + lax.broadcasted_iota(jnp.int32, (1, tile), 1)
            onehot_t = jnp.where(rank_col == slot.astype(F32), 1.0, 0.0).astype(BF16)
            o_ref[...] += _dot(onehot_t, ys_ref[rows, :].astype(BF16))
            return carry

        lax.fori_loop(0, n_tiles, scatter, 0)

    @pl.when(step == pl.num_programs(1) - 1)
    def _():
        o_ref[...] = _layer_norm(alpha * x_ref[...] + o_ref[...], g_ref[...], beta_ref[...])


def _moe_grouped(x, gates_t, wg_bf, wu_bf, wd_bf, g, beta, alpha):
    m, d = x.shape
    n_exp, _, ff = wg_bf.shape
    per = n_exp // N_GROUPS
    chunk = _tile(m, MOE_CHUNK)
    tile = min(chunk, -(-(chunk // N_GROUPS * 9 // 8) // 16) * 16)
    slots = -(-chunk // tile) * tile
    eps = MOE_EXPERTS_PER_STEP
    assert tile % 16 == 0 and 2 * n_exp <= LANES and N_GROUPS <= 8 and per % eps == 0
    gates = gates_t.T
    hi = gates.astype(BF16)
    lo = (gates - hi.astype(F32)).astype(BF16)
    gext = jnp.concatenate([hi, lo, jnp.zeros((m, LANES - 2 * n_exp), BF16)], axis=1)
    counts = jnp.sum(jnp.sum(gates.reshape(m // chunk, chunk, N_GROUPS, per), axis=3) > 0,
                     axis=1).astype(jnp.int32)
    vec = lambda v: v.reshape(1, d)
    row = pl.BlockSpec((1, d), lambda i, e, cnt: (0, 0))
    return pl.pallas_call(
        functools.partial(_moe_grouped_kernel, alpha=alpha, per=per, tile=tile),
        grid_spec=pltpu.PrefetchScalarGridSpec(
            num_scalar_prefetch=1,
            grid=(m // chunk, n_exp // eps),
            in_specs=[pl.BlockSpec((chunk, d), lambda i, e, cnt: (i, 0)),
                      pl.BlockSpec((n_exp, chunk), lambda i, e, cnt: (0, i)),
                      pl.BlockSpec((chunk, LANES), lambda i, e, cnt: (i, 0)),
                      pl.BlockSpec((eps, d, ff), lambda i, e, cnt: (e, 0, 0)),
                      pl.BlockSpec((eps, d, ff), lambda i, e, cnt: (e, 0, 0)),
                      pl.BlockSpec((eps, ff, d), lambda i, e, cnt: (e, 0, 0)),
                      row, row],
            out_specs=pl.BlockSpec((chunk, d), lambda i, e, cnt: (i, 0)),
            scratch_shapes=[pltpu.VMEM((chunk, d), BF16),
                            pltpu.VMEM((slots, d), BF16),
                            pltpu.VMEM((slots, LANES), F32),
                            pltpu.VMEM((slots, d), F32),
                            pltpu.VMEM((8, chunk), F32),
                            pltpu.VMEM((chunk, LANES), F32),
                            pltpu.VMEM((2, chunk, chunk), BF16)]),
        out_shape=jax.ShapeDtypeStruct((m, d), F32),
        compiler_params=_params("arbitrary", "arbitrary", vmem_limit=MOE_VMEM_LIMIT),
        name="moe_grouped",
    )(counts, x, gates_t, gext, wg_bf, wu_bf, wd_bf, vec(g), vec(beta))


def _sb_attend(zs, values, u_neg, mask, overlapped=None):
    n = len(zs)
    first_half, out = [], []
    for t in range(n + 1):
        if t < n:
            z = zs[t]() if callable(zs[t]) else zs[t]
            neg_abs = pltpu.bitcast(pltpu.bitcast(z, jnp.uint32) | jnp.uint32(0x80000000), F32)
            soft = jnp.log(1.0 + jnp.exp2(neg_abs)) * LOG2E
            lb = jnp.minimum(z, 0.0) - soft
            nlk = z - lb
            if mask is not None:
                nlk = jnp.where(mask, nlk, 0.0)
            nlk = nlk.astype(BF16)
            first_half.append((nlk, lb, _dot(nlk, u_neg)))
        if t >= 1:
            nlk, lb, later = first_half[t - 1]
            w = jnp.exp2(lb + later)
            if mask is not None:
                w = jnp.where(mask, w, 0.0)
            out.append((values[t - 1](w.astype(BF16)), later[:, 0:1] - nlk[:, 0:1].astype(F32)))
            if overlapped is not None:
                overlapped[t - 1]()
    return out


def _u_neg(size):
    r = lax.broadcasted_iota(jnp.int32, (size, size), 0)
    c = lax.broadcasted_iota(jnp.int32, (size, size), 1)
    return jnp.where(r > c, -1.0, 0.0).astype(BF16)


def _sb_prompt_kernel(bias_ref, q_ref, kt_ref, vt_ref, o_ref, acc_ref, s_ref, z_ref,
                      *, blk, head_dim):
    hp = pl.program_id(0)
    i = pl.program_id(1)
    heads_per = q_ref.shape[1] // head_dim
    q2 = q_ref[...]
    head_of_lane = lax.broadcasted_iota(jnp.int32, (1, q2.shape[1]), 1) // head_dim
    u_neg = _u_neg(blk)
    part_of_lane = lax.broadcasted_iota(jnp.int32, (blk, SB_BIAS_ROWS), 1)
    qs = []
    for h in range(heads_per):
        bias_lanes = jnp.zeros((blk, SB_BIAS_ROWS), F32)
        for part in range(SB_BIAS_PARTS):
            bias_lanes = jnp.where(part_of_lane == part, bias_ref[hp * heads_per + h, part],
                                   bias_lanes)
        qs.append(jnp.concatenate([jnp.where(head_of_lane == h, q2, jnp.zeros_like(q2)),
                                   bias_lanes.astype(BF16)], axis=1))

    def units_of(key_blocks):
        offs = [pl.multiple_of(kb * blk, blk) for kb in key_blocks]
        return [(o, h) for o in offs for h in range(heads_per)]

    def logit_thunks(key_blocks):
        return [functools.partial(lambda o, h: _dot(qs[h], kt_ref[:, pl.ds(o, blk)]), o, h)
                for o, h in units_of(key_blocks)]

    def visit(zs, key_blocks, mask, first, overlapped=None):
        units = units_of(key_blocks)
        values = [functools.partial(lambda o, w: _dot_nt(w, vt_ref[:, pl.ds(o, blk)]), o)
                  for o, _ in units]
        results = _sb_attend(zs, values, u_neg, mask, overlapped)
        for (pv, total), (o, h) in zip(results, units):
            if first:
                acc_ref[h] = pv
                s_ref[h] = total
            else:
                acc_ref[h] += jnp.exp2(s_ref[h]) * pv
                s_ref[h] += total

    r = lax.broadcasted_iota(jnp.int32, (blk, blk), 0)
    c = lax.broadcasted_iota(jnp.int32, (blk, blk), 1)
    visit(logit_thunks([i]), [i], c < r, True)

    done = 0
    run = 1
    while run < SB_KEY_BLOCKS_PER_STEP:
        top = i - done

        @pl.when(i & run != 0)
        def _(top=top, run=run):
            blocks = [top - 1 - j for j in range(run)]
            visit(logit_thunks(blocks), blocks, None, False)

        done = done + (i & run)
        run *= 2

    n_steps = i // SB_KEY_BLOCKS_PER_STEP
    n_units = SB_KEY_BLOCKS_PER_STEP * heads_per

    def step_blocks(step):
        top = i - i % SB_KEY_BLOCKS_PER_STEP - SB_KEY_BLOCKS_PER_STEP * step
        return [jnp.maximum(top - 1 - j, 0) for j in range(SB_KEY_BLOCKS_PER_STEP)]

    def store_logits(u, thunk):
        z_ref[u] = thunk()

    @pl.when(n_steps > 0)
    def _():
        for u, thunk in enumerate(logit_thunks(step_blocks(0))):
            store_logits(u, thunk)

    def body(step, carry):
        refill = [functools.partial(store_logits, u, thunk)
                  for u, thunk in enumerate(logit_thunks(step_blocks(step + 1)))]
        current = [functools.partial(lambda u: z_ref[u], u) for u in range(n_units)]
        visit(current, step_blocks(step), None, False, refill)
        return carry

    lax.fori_loop(0, n_steps, body, 0)
    out = acc_ref[0]
    for h in range(1, heads_per):
        out = jnp.where(head_of_lane == h, acc_ref[h], out)
    o_ref[...] = out


def _sb_prompt(q_bf, kt_bf, vt_bf, bias2, head_dim):
    n, width = q_bf.shape
    n_groups = width // LANES
    blk = _tile(n, 256)
    parts, rest = [], bias2
    for _ in range(SB_BIAS_PARTS):
        part = rest.astype(BF16).astype(F32)
        parts.append(part)
        rest = rest - part
    bias_parts = jnp.stack(parts, axis=1)
    return pl.pallas_call(
        functools.partial(_sb_prompt_kernel, blk=blk, head_dim=head_dim),
        grid=(n_groups, n // blk),
        in_specs=[pl.BlockSpec(memory_space=pltpu.SMEM),
                  pl.BlockSpec((blk, LANES), lambda p, i: (i, p)),
                  pl.BlockSpec((LANES + SB_BIAS_ROWS, n), lambda p, i: (p, 0)),
                  pl.BlockSpec((LANES, n), lambda p, i: (p, 0))],
        out_specs=pl.BlockSpec((blk, LANES), lambda p, i: (i, p)),
        out_shape=jax.ShapeDtypeStruct((n, width), F32),
        scratch_shapes=[pltpu.VMEM((LANES // head_dim, blk, LANES), F32),
                        pltpu.VMEM((LANES // head_dim, blk, 1), F32),
                        pltpu.VMEM((SB_KEY_BLOCKS_PER_STEP * (LANES // head_dim), blk, blk), F32)],
        compiler_params=_params("parallel", "arbitrary"),
        name="sb_prompt",
    )(bias_parts, q_bf, kt_bf, vt_bf)


def _sb_sample_kernel(pt_ref, qbd_ref, knew_ref, vnew_ref, bias_ref, kpool_ref, vpool_ref,
                      o_ref, kbuf, vbuf, sem, pad_k, pad_v, acc_ref,
                      *, n_batch, n_pages, page, group, sub, t_new):
    b = pl.program_id(0)
    n_steps = n_pages // group
    total_steps = n_batch * n_steps
    ahead = SB_PAGE_SLOTS - 1

    def copies(gstep):
        seq = gstep // n_steps
        first = (n_steps - 1 - gstep % n_steps) * group
        slot = gstep % SB_PAGE_SLOTS
        out = []
        for r in range(group):
            phys = pt_ref[seq, first + r]
            out.append(pltpu.make_async_copy(kpool_ref.at[phys], kbuf.at[slot, r], sem.at[0, slot]))
            out.append(pltpu.make_async_copy(vpool_ref.at[phys], vbuf.at[slot, r], sem.at[1, slot]))
        return out

    @pl.when(b == 0)
    def _():
        for gstep in range(min(ahead, total_steps)):
            for cp in copies(gstep):
                cp.start()

    qbd = qbd_ref[0]
    bias = bias_ref[...]
    rows = qbd.shape[0]

    pad_k[...] = jnp.zeros(pad_k.shape, F32)
    pad_v[...] = jnp.zeros(pad_v.shape, F32)
    pad_k[0:t_new, :] = knew_ref[0]
    pad_v[0:t_new, :] = vnew_ref[0]
    t_of_row = lax.broadcasted_iota(jnp.int32, (rows, page), 0) % SB_ROWS_PER_HEAD
    s_of_col = lax.broadcasted_iota(jnp.int32, (rows, page), 1)
    (pv, s_after), = _sb_attend([_dot_nt(qbd, pad_k[...].astype(BF16)) + bias],
                                [lambda w: _dot(w, pad_v[...].astype(BF16))], _u_neg(page),
                                s_of_col < t_of_row)
    acc_ref[...] = pv

    per_sub = sub // page
    u_neg = _u_neg(sub)

    def body(step, s_after):
        gstep = b * n_steps + step
        slot = gstep % SB_PAGE_SLOTS
        for cp in copies(gstep):
            cp.wait()

        @pl.when(gstep + ahead < total_steps)
        def _():
            for cp in copies(gstep + ahead):
                cp.start()

        def keys_of(buf, first):
            return jnp.concatenate([buf[slot, first + j].astype(BF16) for j in range(per_sub)],
                                   axis=1)

        firsts = list(reversed(range(0, group, per_sub)))
        zs = [_dot(qbd, keys_of(kbuf, f)) + bias for f in firsts]
        values = [functools.partial(lambda f, w: _dot_nt(w, keys_of(vbuf, f)), f) for f in firsts]
        upd = jnp.zeros(acc_ref.shape, F32)
        for pv, total in _sb_attend(zs, values, u_neg, None):
            upd = upd + jnp.exp2(s_after) * pv
            s_after = s_after + total
        acc_ref[...] += upd
        return s_after

    lax.fori_loop(0, n_steps, body, s_after)

    width = acc_ref.shape[1]
    n_heads = rows // SB_ROWS_PER_HEAD
    head_dim = width // n_heads
    head_of_row = lax.broadcasted_iota(jnp.int32, (rows, width), 0) // SB_ROWS_PER_HEAD
    head_of_col = lax.broadcasted_iota(jnp.int32, (rows, width), 1) // head_dim
    own = jnp.where(head_of_row == head_of_col, acc_ref[...], 0.0)
    out = own[0:SB_ROWS_PER_HEAD, :]
    for h in range(1, n_heads):
        out = out + own[h * SB_ROWS_PER_HEAD:(h + 1) * SB_ROWS_PER_HEAD, :]
    o_ref[0] = out[0:t_new, :]


def _sb_sample(q, k_new, v_new, bias2, pool_kt, pool_vt, page_table, q_scale):
    b, t_new, width = q.shape
    n_heads = bias2.shape[0]
    head_dim = width // n_heads
    page = pool_kt.shape[2]
    n_pages = page_table.shape[1]
    group = max(g for g in (4, 2, 1) if n_pages % g == 0)
    sub = min(2, group) * page
    rows = n_heads * SB_ROWS_PER_HEAD
    head_of_col = jnp.arange(width) // head_dim
    qpad = jnp.pad(q * q_scale, ((0, 0), (0, SB_ROWS_PER_HEAD - t_new), (0, 0)))
    qbd = jnp.where(head_of_col[None, None, None, :] == jnp.arange(n_heads)[None, :, None, None],
                    qpad[:, None, :, :], 0.0).reshape(b, rows, width).astype(BF16)
    bias_rows = jnp.repeat(bias2, SB_ROWS_PER_HEAD).reshape(rows, 1)
    return pl.pallas_call(
        functools.partial(_sb_sample_kernel, n_batch=b, n_pages=n_pages, page=page, group=group,
                          sub=sub, t_new=t_new),
        grid_spec=pltpu.PrefetchScalarGridSpec(
            num_scalar_prefetch=1,
            grid=(b,),
            in_specs=[pl.BlockSpec((1, rows, width), lambda i, pt: (i, 0, 0)),
                      pl.BlockSpec((1, t_new, width), lambda i, pt: (i, 0, 0)),
                      pl.BlockSpec((1, t_new, width), lambda i, pt: (i, 0, 0)),
                      pl.BlockSpec((rows, 1), lambda i, pt: (0, 0)),
                      pl.BlockSpec(memory_space=pl.ANY),
                      pl.BlockSpec(memory_space=pl.ANY)],
            out_specs=pl.BlockSpec((1, t_new, width), lambda i, pt: (i, 0, 0)),
            scratch_shapes=[pltpu.VMEM((SB_PAGE_SLOTS, group, width, page), F32),
                            pltpu.VMEM((SB_PAGE_SLOTS, group, width, page), F32),
                            pltpu.SemaphoreType.DMA((2, SB_PAGE_SLOTS)),
                            pltpu.VMEM((page, width), F32),
                            pltpu.VMEM((page, width), F32),
                            pltpu.VMEM((rows, width), F32)]),
        out_shape=jax.ShapeDtypeStruct((b, t_new, width), F32),
        compiler_params=_params("arbitrary"),
        name="sb_sample",
    )(page_table, qbd, k_new, v_new, bias_rows, pool_kt, pool_vt)


def _feature_major(t):
    lead = t.shape[:-3]
    tokens, heads, dim = t.shape[-3:]
    n = len(lead)
    return jnp.transpose(t, (*range(n), n + 1, n + 2, n)).reshape(*lead, heads * dim, tokens)


def _token_major(t, heads):
    lead = t.shape[:-2]
    width, tokens = t.shape[-2:]
    n = len(lead)
    t = t.reshape(*lead, heads, width // heads, tokens)
    return jnp.transpose(t, (*range(n), n + 2, n, n + 1))


def kernel(x_prompt, x_sample, state_conv, cache_sb_k, cache_sb_v, cache_mem_k, cache_mem_v, page_table, mem_prompt, a_w_in, a_b_in, a_conv_w, a_conv_b, a_norm_g, a_norm_b, a_w_out, a_b_out, b_w_in, b_b_in, b_sb_bias, b_w_out, b_b_out, w_mem_kv, ln_mix_g, ln_mix_b, ln_ffn_g, ln_ffn_b, w_router, b_router, w_exp_gate, w_exp_up, w_exp_down):
    depth = w_mem_kv.shape[0]
    n_mixers = 2
    alpha = (2.0 * depth) ** 0.25
    bp, seq, d = x_prompt.shape
    bs, t_new, _ = x_sample.shape
    mem_heads, mem_hd = cache_mem_k.shape[-2:]
    mem_w = mem_heads * mem_hd
    sb_heads, sb_hd = cache_sb_k.shape[-2:]
    sb_w = sb_heads * sb_hd
    conv_ch = a_conv_w.shape[-1]
    mem_tokens = mem_prompt.shape[1]
    assert bp == 1
    sb_q_scale = float(sb_hd) ** -0.5 * LOG2E

    y_p = x_prompt.reshape(seq, d)
    y_s = x_sample.reshape(bs * t_new, d)
    mem_p = mem_prompt.reshape(mem_tokens, d)
    conv_p, conv_s, sbk_p, sbv_p, sbk_s, sbv_s, memk_p, memv_p = ([] for _ in range(8))

    for layer in range(depth):
        kvt = _linear_t(mem_p, w_mem_kv[layer].T.astype(BF16), name="mem_kv")
        mkt_p = kvt[:mem_w][None]
        mvt_p = kvt[mem_w:][None]
        memk_p.append(_token_major(mkt_p, mem_heads))
        memv_p.append(_token_major(mvt_p, mem_heads))
        mkt_s = _feature_major(cache_mem_k[layer])
        mvt_s = _feature_major(cache_mem_v[layer])
        i = layer // n_mixers
        if layer % n_mixers == 0:
            w_in = a_w_in[i].astype(BF16)
            proj_p = _linear(y_p, w_in, a_b_in[i], name="conv_in_proj")
            proj_s = _linear(y_s, w_in, a_b_in[i], name="conv_in_proj_s")
            mix_p, st_p = _conv_prompt(proj_p, a_conv_w[i], a_conv_b[i], a_norm_g[i], a_norm_b[i])
            conv_p.append(st_p[None])
            proj_tm = jnp.transpose(proj_s.reshape(bs, t_new, -1), (1, 0, 2))
            mix_s, st_s = _conv_sample(proj_tm, jnp.transpose(state_conv[i], (1, 0, 2)),
                                       a_conv_w[i], a_conv_b[i], a_norm_g[i], a_norm_b[i])
            mix_s = jnp.transpose(mix_s, (1, 0, 2)).reshape(bs * t_new, conv_ch)
            conv_s.append(jnp.transpose(st_s, (1, 0, 2)))
            q_col = (2 * conv_ch) // mem_w
            assert q_col * mem_w == 2 * conv_ch
            qm_p, qm_s = proj_p[None], proj_s.reshape(bs, t_new, -1)
            w_out, b_out = a_w_out[i], a_b_out[i]
        else:
            bias2 = b_sb_bias[i] * LOG2E
            q_bf, kt, vt, kt_bf, vt_bf, qm = _sb_in_proj(y_p, b_w_in[i], b_b_in[i], sb_w, sb_q_scale)
            sbk_p.append(_token_major(kt[None], sb_heads))
            sbv_p.append(_token_major(vt[None], sb_heads))
            mix_p = _sb_prompt(q_bf, kt_bf, vt_bf, bias2, sb_hd)
            proj_s = _linear(y_s, b_w_in[i].astype(BF16), b_b_in[i], name="sb_in_proj_s")
            proj_s3 = proj_s.reshape(bs, t_new, -1)
            k_s = proj_s3[:, :, sb_w:2 * sb_w]
            v_s = proj_s3[:, :, 2 * sb_w:3 * sb_w]
            sbk_s.append(k_s.reshape(bs, t_new, sb_heads, sb_hd))
            sbv_s.append(v_s.reshape(bs, t_new, sb_heads, sb_hd))
            mix_s = _sb_sample(proj_s3[:, :, :sb_w], k_s, v_s, bias2,
                               _feature_major(cache_sb_k[i]), _feature_major(cache_sb_v[i]),
                               page_table, sb_q_scale)
            mix_s = mix_s.reshape(bs * t_new, sb_w)
            q_col = (3 * sb_w) // mem_w
            assert q_col * mem_w == 3 * sb_w
            qm_p, qm_s = qm[None], proj_s3
            q_col_p = 0
            w_out, b_out = b_w_out[i], b_b_out[i]

        mem_out_p = _mem_attn(qm_p, q_col if layer % n_mixers == 0 else q_col_p, mkt_p, mvt_p,
                              mem_heads)
        mem_out_s = _mem_attn(qm_s, q_col, mkt_s, mvt_s, mem_heads)
        w_out_bf = w_out.astype(BF16)
        y_p = _out_proj(y_p, mix_p, mem_out_p.reshape(seq, mem_w), w_out_bf, b_out,
                        ln_mix_g[layer], ln_mix_b[layer], alpha)
        y_s = _out_proj(y_s, mix_s, mem_out_s.reshape(bs * t_new, mem_w), w_out_bf, b_out,
                        ln_mix_g[layer], ln_mix_b[layer], alpha)

        wg, wu, wd = (w[layer].astype(BF16) for w in (w_exp_gate, w_exp_up, w_exp_down))
        y_p = _moe_grouped(y_p, _router(y_p, w_router, b_router), wg, wu, wd,
                           ln_ffn_g[layer], ln_ffn_b[layer], alpha)
        y_s = _moe(y_s, _router(y_s, w_router, b_router).T, wg, wu, wd,
                   ln_ffn_g[layer], ln_ffn_b[layer], alpha)

    return (y_p.reshape(bp, seq, d), y_s.reshape(bs, t_new, d),
            jnp.stack(conv_p), jnp.stack(conv_s), jnp.stack(sbk_p), jnp.stack(sbv_p),
            jnp.stack(sbk_s), jnp.stack(sbv_s), jnp.stack(memk_p), jnp.stack(memv_p))
```

```python
import functools
import math

import jax
import jax.numpy as jnp
from jax import lax
from jax.experimental import pallas as pl
from jax.experimental.pallas import tpu as pltpu

LN_EPS = 1e-5
N_GROUPS = 4
TOP_K = 2
CONV_HALO = 32
LANES = 128
MATMUL_ROWS = 512
SB_ROWS_PER_HEAD = 8
SB_KEY_BLOCKS_PER_STEP = 8
SB_PAGE_SLOTS = 3
SB_BIAS_ROWS = 16
SB_BIAS_PARTS = 3
MOE_CHUNK = 1024
MOE_EXPERTS_PER_STEP = 4
VMEM_LIMIT = 48 * 1024 * 1024
MOE_VMEM_LIMIT = 60 * 1024 * 1024
LOG2E = math.log2(math.e)

BF16 = jnp.bfloat16
F32 = jnp.float32


def _tile(n, pref):
    if n <= pref:
        return n
    t = pref - pref % 8
    while t >= 8:
        if n % t == 0:
            return t
        t -= 8
    return n


def _params(*sem, vmem_limit=VMEM_LIMIT):
    return pltpu.CompilerParams(dimension_semantics=sem, vmem_limit_bytes=vmem_limit)


def _layer_norm(x, g, b):
    mu = jnp.mean(x, axis=-1, keepdims=True)
    xc = x - mu
    var = jnp.mean(xc * xc, axis=-1, keepdims=True)
    return xc * lax.rsqrt(var + LN_EPS) * g + b


def _dot(a, b):
    return jnp.dot(a, b, preferred_element_type=F32)


def _dot_nt(a, b):
    return lax.dot_general(a, b, (((1,), (1,)), ((), ())), preferred_element_type=F32)


def _linear_kernel(x_ref, w_ref, b_ref, o_ref):
    o_ref[...] = _dot(x_ref[...].astype(BF16), w_ref[...]) + b_ref[...]


def _linear(x, w_bf, b, *, name="linear"):
    m, k = x.shape
    n = w_bf.shape[1]
    tm = _tile(m, MATMUL_ROWS)
    return pl.pallas_call(
        _linear_kernel,
        grid=(m // tm,),
        in_specs=[pl.BlockSpec((tm, k), lambda i: (i, 0)),
                  pl.BlockSpec((k, n), lambda i: (0, 0)),
                  pl.BlockSpec((1, n), lambda i: (0, 0))],
        out_specs=pl.BlockSpec((tm, n), lambda i: (i, 0)),
        out_shape=jax.ShapeDtypeStruct((m, n), F32),
        compiler_params=_params("parallel"),
        name=name,
    )(x, w_bf, b.reshape(1, n))


def _linear_t_kernel(x_ref, wt_ref, o_ref):
    o_ref[...] = _dot_nt(wt_ref[...], x_ref[...].astype(BF16))


def _linear_t(x, wt_bf, *, name="linear_t"):
    m, k = x.shape
    n = wt_bf.shape[0]
    tm = _tile(m, 256)
    return pl.pallas_call(
        _linear_t_kernel,
        grid=(m // tm,),
        in_specs=[pl.BlockSpec((tm, k), lambda i: (i, 0)),
                  pl.BlockSpec((n, k), lambda i: (0, 0))],
        out_specs=pl.BlockSpec((n, tm), lambda i: (0, i)),
        out_shape=jax.ShapeDtypeStruct((n, m), F32),
        compiler_params=_params("parallel"),
        name=name,
    )(x, wt_bf)


def _sb_in_proj_kernel(x_ref, wq_ref, wkt_ref, wvt_ref, wm_ref, bq_ref, bk_ref, bv_ref, bm_ref,
                       q_ref, kt_ref, vt_ref, ktb_ref, vtb_ref, qm_ref, *, q_scale):
    xb = x_ref[...].astype(BF16)
    q_ref[...] = ((_dot(xb, wq_ref[...]) + bq_ref[...]) * q_scale).astype(BF16)
    kt = _dot_nt(wkt_ref[...], xb) + bk_ref[...]
    kt_ref[...] = kt
    ones = jnp.ones((SB_BIAS_ROWS, kt.shape[1]), BF16)
    for p in range(kt.shape[0] // LANES):
        base = p * (LANES + SB_BIAS_ROWS)
        ktb_ref[base:base + LANES, :] = kt[p * LANES:(p + 1) * LANES].astype(BF16)
        ktb_ref[base + LANES:base + LANES + SB_BIAS_ROWS, :] = ones
    vt = _dot_nt(wvt_ref[...], xb) + bv_ref[...]
    vt_ref[...] = vt
    vtb_ref[...] = vt.astype(BF16)
    qm_ref[...] = _dot(xb, wm_ref[...]) + bm_ref[...]


def _sb_in_proj(x, w_in, b_in, sb_w, q_scale):
    n, d = x.shape
    wm = w_in.shape[1] - 3 * sb_w
    tm = _tile(n, MATMUL_ROWS)
    kt_rows =sb_w // LANES * (LANES + SB_BIAS_ROWS)
    wq = w_in[:, :sb_w].astype(BF16)
    wkt = w_in[:, sb_w:2 * sb_w].T.astype(BF16)
    wvt = w_in[:, 2 * sb_w:3 * sb_w].T.astype(BF16)
    wmem = w_in[:, 3 * sb_w:].astype(BF16)
    const = lambda shape: pl.BlockSpec(shape, lambda i: (0, 0))
    return pl.pallas_call(
        functools.partial(_sb_in_proj_kernel, q_scale=q_scale),
        grid=(n // tm,),
        in_specs=[pl.BlockSpec((tm, d), lambda i: (i, 0)),
                  const((d, sb_w)), const((sb_w, d)), const((sb_w, d)), const((d, wm)),
                  const((1, sb_w)), const((sb_w, 1)), const((sb_w, 1)), const((1, wm))],
        out_specs=[pl.BlockSpec((tm, sb_w), lambda i: (i, 0)),
                   pl.BlockSpec((sb_w, tm), lambda i: (0, i)),
                   pl.BlockSpec((sb_w, tm), lambda i: (0, i)),
                   pl.BlockSpec((kt_rows, tm), lambda i: (0, i)),
                   pl.BlockSpec((sb_w, tm), lambda i: (0, i)),
                   pl.BlockSpec((tm, wm), lambda i: (i, 0))],
        out_shape=[jax.ShapeDtypeStruct((n, sb_w), BF16),
                   jax.ShapeDtypeStruct((sb_w, n), F32),
                   jax.ShapeDtypeStruct((sb_w, n), F32),
                   jax.ShapeDtypeStruct((kt_rows, n), BF16),
                   jax.ShapeDtypeStruct((sb_w, n), BF16),
                   jax.ShapeDtypeStruct((n, wm), F32)],
        compiler_params=_params("parallel"),
        name="sb_in_proj",
    )(x, wq, wkt, wvt, wmem,
      b_in[:sb_w].reshape(1, sb_w), b_in[sb_w:2 * sb_w].reshape(sb_w, 1),
      b_in[2 * sb_w:3 * sb_w].reshape(sb_w, 1), b_in[3 * sb_w:].reshape(1, wm))


def _conv_prompt_kernel(a_ref, gate_ref, w_ref, cb_ref, ng_ref, nb_ref,
                        mix_ref, st_ref, f_ref, g_ref, *, tile, width, chunk):
    i = pl.program_id(0)
    halo = CONV_HALO
    sub = 8

    @pl.when(i == 0)
    def _():
        f_ref[0:halo, :] = jnp.zeros((halo, f_ref.shape[1]), F32)

    f_ref[halo:halo + tile, :] = a_ref[...] * jax.nn.sigmoid(gate_ref[...])
    shifted_rows = g_ref.shape[1]
    for b in range(1, sub):
        g_ref[b - 1] = f_ref[b:b + shifted_rows, :]
    first = halo - (width - 1)
    for c in range(tile // chunk):
        y = jnp.zeros((chunk, f_ref.shape[1]), F32) + cb_ref[...]
        for k in range(width):
            b = (first + k) % sub
            row = c * chunk + (first + k) - b
            src = f_ref[row:row + chunk, :] if b == 0 else g_ref[b - 1, row:row + chunk, :]
            y = y + src * w_ref[k:k + 1, :]
        y = _layer_norm(y, ng_ref[...], nb_ref[...])
        mix_ref[c * chunk:(c + 1) * chunk, :] = y * jax.nn.sigmoid(y)

    @pl.when(i == pl.num_programs(0) - 1)
    def _():
        st_ref[...] = f_ref[halo + tile - (width - 1):halo + tile, :]

    f_ref[0:halo, :] = f_ref[tile:tile + halo, :]


def _conv_prompt(proj, conv_w, conv_b, norm_g, norm_b):
    n = proj.shape[0]
    width, c = conv_w.shape
    tile = _tile(n, 256)
    chunk = _tile(tile, 32)
    vec = lambda v: v.reshape(1, c)
    row = pl.BlockSpec((1, c), lambda i: (0, 0))
    return pl.pallas_call(
        functools.partial(_conv_prompt_kernel, tile=tile, width=width, chunk=chunk),
        grid=(n // tile,),
        in_specs=[pl.BlockSpec((tile, c), lambda i: (i, 0)),
                  pl.BlockSpec((tile, c), lambda i: (i, 1)),
                  pl.BlockSpec((width, c), lambda i: (0, 0)),
                  row, row, row],
        out_specs=[pl.BlockSpec((tile, c), lambda i: (i, 0)),
                   pl.BlockSpec((width - 1, c), lambda i: (0, 0))],
        out_shape=[jax.ShapeDtypeStruct((n, c), F32),
                   jax.ShapeDtypeStruct((width - 1, c), F32)],
        scratch_shapes=[pltpu.VMEM((tile + CONV_HALO, c), F32),
                        pltpu.VMEM((7, tile + CONV_HALO - 8, c), F32)],
        compiler_params=_params("arbitrary"),
        name="conv_prompt",
    )(proj, proj, conv_w, vec(conv_b), vec(norm_g), vec(norm_b))


def _conv_sample_kernel(a_ref, gate_ref, past_ref, w_ref, cb_ref, ng_ref, nb_ref,
                        mix_ref, st_ref, *, t_new, width):
    hist = width - 1
    new = [a_ref[t] * jax.nn.sigmoid(gate_ref[t]) for t in range(t_new)]
    full = lambda j: past_ref[j] if j < hist else new[j - hist]
    for t in range(t_new):
        y = jnp.zeros(new[0].shape, F32) + cb_ref[...]
        for k in range(width):
            y = y + full(t + k) * w_ref[k:k + 1, :]
        y = _layer_norm(y, ng_ref[...], nb_ref[...])
        mix_ref[t] = y * jax.nn.sigmoid(y)
    for j in range(hist):
        st_ref[j] = full(j + t_new)


def _conv_sample(proj_tm, past_tm, conv_w, conv_b, norm_g, norm_b):
    t_new, batch, _ = proj_tm.shape
    width, c = conv_w.shape
    vec = lambda v: v.reshape(1, c)
    row = pl.BlockSpec((1, c), lambda i: (0, 0))
    return pl.pallas_call(
        functools.partial(_conv_sample_kernel, t_new=t_new, width=width),
        grid=(1,),
        in_specs=[pl.BlockSpec((t_new, batch, c), lambda i: (0, 0, 0)),
                  pl.BlockSpec((t_new, batch, c), lambda i: (0, 0, 1)),
                  pl.BlockSpec((width - 1, batch, c), lambda i: (0, 0, 0)),
                  pl.BlockSpec((width, c), lambda i: (0, 0)),
                  row, row, row],
        out_specs=[pl.BlockSpec((t_new, batch, c), lambda i: (0, 0, 0)),
                   pl.BlockSpec((width - 1, batch, c), lambda i: (0, 0, 0))],
        out_shape=[jax.ShapeDtypeStruct((t_new, batch, c), F32),
                   jax.ShapeDtypeStruct((width - 1, batch, c), F32)],
        compiler_params=_params("arbitrary"),
        name="conv_sample",
    )(proj_tm, proj_tm, past_tm, conv_w, vec(conv_b), vec(norm_g), vec(norm_b))


def _mem_attn_kernel(q_ref, kt_ref, vt_ref, o_ref, *, heads, scale):
    q = q_ref[0]
    kt = kt_ref[0].astype(BF16)
    vt = vt_ref[0].astype(BF16)
    width = q.shape[-1]
    head_of_lane = lax.broadcasted_iota(jnp.int32, (1, width), 1) // (width // heads)
    out = jnp.zeros(q.shape, F32)
    for h in range(heads):
        sel = head_of_lane == h
        s = _dot(jnp.where(sel, q, 0.0).astype(BF16), kt) * scale
        p = jnp.exp(s - jnp.max(s, axis=-1, keepdims=True))
        p = p / jnp.sum(p, axis=-1, keepdims=True)
        out = out + jnp.where(sel, _dot_nt(p.astype(BF16), vt), 0.0)
    o_ref[0] = out


def _mem_attn(q_src, col_block, mem_kt, mem_vt, heads):
    b, t, _ = q_src.shape
    width, m = mem_kt.shape[1:]
    tq = _tile(t, 512)
    scale = float(width // heads) ** -0.5
    return pl.pallas_call(
        functools.partial(_mem_attn_kernel, heads=heads, scale=scale),
        grid=(b, t // tq),
        in_specs=[pl.BlockSpec((1, tq, width), lambda i, j: (i, j, col_block)),
                  pl.BlockSpec((1, width, m), lambda i, j: (i, 0, 0)),
                  pl.BlockSpec((1, width, m), lambda i, j: (i, 0, 0))],
        out_specs=pl.BlockSpec((1, tq, width), lambda i, j: (i, j, 0)),
        out_shape=jax.ShapeDtypeStruct((b, t, width), F32),
        compiler_params=_params("parallel", "parallel"),
        name="mem_attn",
    )(q_src, mem_kt, mem_vt)


def _out_proj_kernel(x_ref, mix_ref, mem_ref, wa_ref, wb_ref, b_ref, g_ref, beta_ref,
                     o_ref, *, alpha):
    acc = _dot(mix_ref[...].astype(BF16), wa_ref[...])
    acc = acc + _dot(mem_ref[...].astype(BF16), wb_ref[...]) + b_ref[...]
    o_ref[...] = _layer_norm(alpha * x_ref[...] + acc, g_ref[...], beta_ref[...])


def _out_proj(x, mix, mem, w_out_bf, b_out, g, beta, alpha):
    m, d = x.shape
    cm = mix.shape[1]
    cw = mem.shape[1]
    tm = _tile(m, MATMUL_ROWS)
    vec = lambda v: v.reshape(1, d)
    row = pl.BlockSpec((1, d), lambda i: (0, 0))
    return pl.pallas_call(
        functools.partial(_out_proj_kernel, alpha=alpha),
        grid=(m // tm,),
        in_specs=[pl.BlockSpec((tm, d), lambda i: (i, 0)),
                  pl.BlockSpec((tm, cm), lambda i: (i, 0)),
                  pl.BlockSpec((tm, cw), lambda i: (i, 0)),
                  pl.BlockSpec((cm, d), lambda i: (0, 0)),
                  pl.BlockSpec((cw, d), lambda i: (0, 0)),
                  row, row, row],
        out_specs=pl.BlockSpec((tm, d), lambda i: (i, 0)),
        out_shape=jax.ShapeDtypeStruct((m, d), F32),
        compiler_params=_params("parallel"),
        name="out_proj_ln",
    )(x, mix, mem, w_out_bf[:cm], w_out_bf[cm:], vec(b_out), vec(g), vec(beta))


def _top2(rows):
    best, bi = rows[0], jnp.zeros(rows[0].shape, jnp.int32)
    for j in range(1, len(rows)):
        upd = rows[j] > best
        best = jnp.where(upd, rows[j], best)
        bi = jnp.where(upd, j, bi)
    sec, si = None, None
    for j in range(len(rows)):
        cand = jnp.where(bi == j, -jnp.inf, rows[j])
        if sec is None:
            sec, si = cand, jnp.zeros(cand.shape, jnp.int32)
        else:
            upd = cand > sec
            sec = jnp.where(upd, cand, sec)
            si = jnp.where(upd, j, si)
    return best, sec, bi, si


def _router_kernel(x_ref, wt_ref, b_ref, o_ref, *, n_groups):
    logits = lax.dot_general(wt_ref[...], x_ref[...], (((1,), (1,)), ((), ())),
                             preferred_element_type=F32,
                             precision=lax.Precision.HIGHEST) + b_ref[...]
    n_exp = logits.shape[0]
    per = n_exp // n_groups
    e = jnp.exp(logits - jnp.max(logits, axis=0, keepdims=True))
    probs = e / jnp.sum(e, axis=0, keepdims=True)
    picks = [_top2([probs[g * per + j:g * per + j + 1, :] for j in range(per)])
             for g in range(n_groups)]
    score = picks[0][0] + picks[0][1]
    sel = picks[0]
    g_sel = jnp.zeros(score.shape, jnp.int32)
    for g in range(1, n_groups):
        sc = picks[g][0] + picks[g][1]
        upd = sc > score
        score = jnp.where(upd, sc, score)
        g_sel = jnp.where(upd, g, g_sel)
        sel = tuple(jnp.where(upd, a, b) for a, b in zip(picks[g], sel))
    p1, p2, i1, i2 = sel
    denom = p1 + p2
    e1 = g_sel * per + i1
    e2 = g_sel * per + i2
    eidx = lax.broadcasted_iota(jnp.int32, logits.shape, 0)
    o_ref[...] = (jnp.where(eidx == e1, p1 / denom, 0.0)
                  + jnp.where(eidx == e2, p2 / denom, 0.0))


def _router(x, w_router, b_router):
    m, d = x.shape
    n_exp = w_router.shape[1]
    tm = _tile(m, 512)
    return pl.pallas_call(
        functools.partial(_router_kernel, n_groups=N_GROUPS),
        grid=(m // tm,),
        in_specs=[pl.BlockSpec((tm, d), lambda i: (i, 0)),
                  pl.BlockSpec((n_exp, d), lambda i: (0, 0)),
                  pl.BlockSpec((n_exp, 1), lambda i: (0, 0))],
        out_specs=pl.BlockSpec((n_exp, tm), lambda i: (0, i)),
        out_shape=jax.ShapeDtypeStruct((n_exp, m), F32),
        compiler_params=_params("parallel"),
        name="router",
    )(x, w_router.T, b_router.reshape(n_exp, 1))


def _moe_kernel(x_ref, gates_ref, wg_ref, wu_ref, wd_ref, g_ref, beta_ref, o_ref,
                xb_ref, acc_ref, *, alpha):
    e = pl.program_id(1)

    @pl.when(e == 0)
    def _():
        xb_ref[...] = x_ref[...].astype(BF16)
        acc_ref[...] = jnp.zeros(acc_ref.shape, F32)

    gates = gates_ref[...]
    lane = lax.broadcasted_iota(jnp.int32, gates.shape, 1)
    gate = jnp.sum(jnp.where(lane == e, gates, 0.0), axis=1, keepdims=True)
    xb = xb_ref[...]
    hg = _dot(xb, wg_ref[0])
    hu = _dot(xb, wu_ref[0])
    h = hg * jax.nn.sigmoid(hg) * hu * gate
    acc_ref[...] += _dot(h.astype(BF16), wd_ref[0])

    @pl.when(e == pl.num_programs(1) - 1)
    def _():
        o_ref[...] = _layer_norm(alpha * x_ref[...] + acc_ref[...], g_ref[...], beta_ref[...])


def _moe(x, gates, wg_bf, wu_bf, wd_bf, g, beta, alpha):
    m, d = x.shape
    n_exp, _, ff = wg_bf.shape
    tm = _tile(m, 512)
    vec = lambda v: v.reshape(1, d)
    row = pl.BlockSpec((1, d), lambda i, e: (0, 0))
    return pl.pallas_call(
        functools.partial(_moe_kernel, alpha=alpha),
        grid=(m // tm, n_exp),
        in_specs=[pl.BlockSpec((tm, d), lambda i, e: (i, 0)),
                  pl.BlockSpec((tm, n_exp), lambda i, e: (i, 0)),
                  pl.BlockSpec((1, d, ff), lambda i, e: (e, 0, 0)),
                  pl.BlockSpec((1, d, ff), lambda i, e: (e, 0, 0)),
                  pl.BlockSpec((1, ff, d), lambda i, e: (e, 0, 0)),
                  row, row],
        out_specs=pl.BlockSpec((tm, d), lambda i, e: (i, 0)),
        out_shape=jax.ShapeDtypeStruct((m, d), F32),
        scratch_shapes=[pltpu.VMEM((tm, d), BF16), pltpu.VMEM((tm, d), F32)],
        compiler_params=_params("parallel", "arbitrary"),
        name="moe",
    )(x, gates, wg_bf, wu_bf, wd_bf, vec(g), vec(beta))


def _moe_grouped_kernel(cnt_ref, x_ref, gt_ref, gext_ref, wg_ref, wu_ref, wd_ref, g_ref, beta_ref,
                        o_ref, xb_ref, xs_ref, gs_ref, ys_ref, rrow_ref, rcol_ref, before_ref,
                        *, alpha, per, tile):
    c = pl.program_id(0)
    step = pl.program_id(1)
    e = step * MOE_EXPERTS_PER_STEP
    g = e // per
    j = e % per
    chunk, d = x_ref.shape
    n_exp = gt_ref.shape[0]
    n_tiles = (cnt_ref[c, g] + tile - 1) // tile

    @pl.when((c == 0) & (e == 0))
    def _():
        r = lax.broadcasted_iota(jnp.int32, (chunk, chunk), 0)
        cc = lax.broadcasted_iota(jnp.int32, (chunk, chunk), 1)
        before_ref[0] = jnp.where(r < cc, 1.0, 0.0).astype(BF16)
        before_ref[1] = jnp.where(cc < r, 1.0, 0.0).astype(BF16)

    @pl.when(e == 0)
    def _():
        xb_ref[...] = x_ref[...].astype(BF16)
        o_ref[...] = jnp.zeros(o_ref.shape, F32)
        gt = gt_ref[...]
        row8 = lax.broadcasted_iota(jnp.int32, (8, chunk), 0)
        member = jnp.zeros((8, chunk), F32)
        for grp in range(n_exp // per):
            in_grp = jnp.sum(gt[grp * per:(grp + 1) * per, :], axis=0, keepdims=True) > 0.0
            member = jnp.where((row8 == grp) & in_grp, 1.0, member)
        rank = _dot(member.astype(BF16), before_ref[0])
        rrow_ref[...] = jnp.where(member > 0.0, rank, -1.0)
        lane_l = lax.broadcasted_iota(jnp.int32, (LANES, LANES), 0)
        lane_g = lax.broadcasted_iota(jnp.int32, (LANES, LANES), 1)
        fold = jnp.where((lane_l < n_exp) & (lane_l // per == lane_g), 1.0, 0.0).astype(BF16)
        picked = jnp.where(gext_ref[...].astype(F32) > 0.0, 1.0, 0.0).astype(BF16)
        member_c = jnp.where(_dot(picked, fold) > 0.0, 1.0, 0.0)
        rank_c = _dot(before_ref[1], member_c.astype(BF16))
        rcol_ref[...] = jnp.where(member_c > 0.0, rank_c, -1.0)

    @pl.when(j == 0)
    def _():
        rank_row = rrow_ref[pl.ds(g, 1), :]

        def gather(t, carry):
            rows = pl.ds(pl.multiple_of(t * tile, 16), tile)
            slot = t * tile + lax.broadcasted_iota(jnp.int32, (tile, 1), 0)
            onehot = jnp.where(rank_row == slot.astype(F32), 1.0, 0.0).astype(BF16)
            xs_ref[rows, :] = _dot(onehot, xb_ref[...]).astype(BF16)
            gs_ref[rows, :] = _dot(onehot, gext_ref[...])
            ys_ref[rows, :] = jnp.zeros((tile, d), F32)
            return carry

        lax.fori_loop(0, n_tiles, gather, 0)

    lane = lax.broadcasted_iota(jnp.int32, (tile, LANES), 1)

    def experts(t, carry):
        rows = pl.ds(pl.multiple_of(t * tile, 16), tile)
        xs = xs_ref[rows, :]
        gs = gs_ref[rows, :]
        out = jnp.zeros((tile, d), F32)
        for k in range(MOE_EXPERTS_PER_STEP):
            gate_lanes = (lane == e + k) | (lane == e + k + n_exp)
            gate = jnp.sum(jnp.where(gate_lanes, gs, 0.0), axis=1, keepdims=True)
            hg = _dot(xs, wg_ref[k])
            hu = _dot(xs, wu_ref[k])
            h = hg * jax.nn.sigmoid(hg) * hu * gate
            out = out + _dot(h.astype(BF16), wd_ref[k])
        ys_ref[rows, :] += out
        return carry

    lax.fori_loop(0, n_tiles, experts, 0)

    @pl.when(j + MOE_EXPERTS_PER_STEP == per)
    def _():
        lane_c = lax.broadcasted_iota(jnp.int32, (chunk, LANES), 1)
        rank_col = jnp.sum(jnp.where(lane_c == g, rcol_ref[...], 0.0), axis=1, keepdims=True)

        def scatter(t, carry):
            rows = pl.ds(pl.multiple_of(t * tile, 16), tile)
            slot = t * tile + lax.broadcasted_iota(jnp.int32, (1, tile), 1)
            onehot_t = jnp.where(rank_col == slot.astype(F32), 1.0, 0.0).astype(BF16)
            o_ref[...] += _dot(onehot_t, ys_ref[rows, :].astype(BF16))
            return carry

        lax.fori_loop(0, n_tiles, scatter, 0)

    @pl.when(step == pl.num_programs(1) - 1)
    def _():
        o_ref[...] = _layer_norm(alpha * x_ref[...] + o_ref[...], g_ref[...], beta_ref[...])


def _moe_grouped(x, gates_t, wg_bf, wu_bf, wd_bf, g, beta, alpha):
    m, d = x.shape
    n_exp, _, ff = wg_bf.shape
    per = n_exp // N_GROUPS
    chunk = _tile(m, MOE_CHUNK)
    tile = min(chunk, -(-(chunk // N_GROUPS * 9 // 8) // 16) * 16)
    slots = -(-chunk // tile) * tile
    eps = MOE_EXPERTS_PER_STEP
    assert tile % 16 == 0 and 2 * n_exp <= LANES and N_GROUPS <= 8 and per % eps == 0
    gates = gates_t.T
    hi = gates.astype(BF16)
    lo = (gates - hi.astype(F32)).astype(BF16)
    gext = jnp.concatenate([hi, lo, jnp.zeros((m, LANES - 2 * n_exp), BF16)], axis=1)
    counts = jnp.sum(jnp.sum(gates.reshape(m // chunk, chunk, N_GROUPS, per), axis=3) > 0,
                     axis=1).astype(jnp.int32)
    vec = lambda v: v.reshape(1, d)
    row = pl.BlockSpec((1, d), lambda i, e, cnt: (0, 0))
    return pl.pallas_call(
        functools.partial(_moe_grouped_kernel, alpha=alpha, per=per, tile=tile),
        grid_spec=pltpu.PrefetchScalarGridSpec(
            num_scalar_prefetch=1,
            grid=(m // chunk, n_exp // eps),
            in_specs=[pl.BlockSpec((chunk, d), lambda i, e, cnt: (i, 0)),
                      pl.BlockSpec((n_exp, chunk), lambda i, e, cnt: (0, i)),
                      pl.BlockSpec((chunk, LANES), lambda i, e, cnt: (i, 0)),
                      pl.BlockSpec((eps, d, ff), lambda i, e, cnt: (e, 0, 0)),
                      pl.BlockSpec((eps, d, ff), lambda i, e, cnt: (e, 0, 0)),
                      pl.BlockSpec((eps, ff, d), lambda i, e, cnt: (e, 0, 0)),
                      row, row],
            out_specs=pl.BlockSpec((chunk, d), lambda i, e, cnt: (i, 0)),
            scratch_shapes=[pltpu.VMEM((chunk, d), BF16),
                            pltpu.VMEM((slots, d), BF16),
                            pltpu.VMEM((slots, LANES), F32),
                            pltpu.VMEM((slots, d), F32),
                            pltpu.VMEM((8, chunk), F32),
                            pltpu.VMEM((chunk, LANES), F32),
                            pltpu.VMEM((2, chunk, chunk), BF16)]),
        out_shape=jax.ShapeDtypeStruct((m, d), F32),
        compiler_params=_params("arbitrary", "arbitrary", vmem_limit=MOE_VMEM_LIMIT),
        name="moe_grouped",
    )(counts, x, gates_t, gext, wg_bf, wu_bf, wd_bf, vec(g), vec(beta))


def _sb_attend(zs, values, u_neg, mask, overlapped=None):
    n = len(zs)
    first_half, out = [], []
    for t in range(n + 1):
        if t < n:
            z = zs[t]() if callable(zs[t]) else zs[t]
            neg_abs = pltpu.bitcast(pltpu.bitcast(z, jnp.uint32) | jnp.uint32(0x80000000), F32)
            soft = jnp.log(1.0 + jnp.exp2(neg_abs)) * LOG2E
            lb = jnp.minimum(z, 0.0) - soft
            nlk = z - lb
            if mask is not None:
                nlk = jnp.where(mask, nlk, 0.0)
            nlk = nlk.astype(BF16)
            first_half.append((nlk, lb, _dot(nlk, u_neg)))
        if t >= 1:
            nlk, lb, later = first_half[t - 1]
            w = jnp.exp2(lb + later)
            if mask is not None:
                w = jnp.where(mask, w, 0.0)
            out.append((values[t - 1](w.astype(BF16)), later[:, 0:1] - nlk[:, 0:1].astype(F32)))
            if overlapped is not None:
                overlapped[t - 1]()
    return out


def _u_neg(size):
    r = lax.broadcasted_iota(jnp.int32, (size, size), 0)
    c = lax.broadcasted_iota(jnp.int32, (size, size), 1)
    return jnp.where(r > c, -1.0, 0.0).astype(BF16)


def _sb_prompt_kernel(bias_ref, q_ref, kt_ref, vt_ref, o_ref, acc_ref, s_ref, z_ref,
                      *, blk, head_dim):
    hp = pl.program_id(0)
    i = pl.program_id(1)
    heads_per = q_ref.shape[1] // head_dim
    q2 = q_ref[...]
    head_of_lane = lax.broadcasted_iota(jnp.int32, (1, q2.shape[1]), 1) // head_dim
    u_neg = _u_neg(blk)
    part_of_lane = lax.broadcasted_iota(jnp.int32, (blk, SB_BIAS_ROWS), 1)
    qs = []
    for h in range(heads_per):
        bias_lanes = jnp.zeros((blk, SB_BIAS_ROWS), F32)
        for part in range(SB_BIAS_PARTS):
            bias_lanes = jnp.where(part_of_lane == part, bias_ref[hp * heads_per + h, part],
                                   bias_lanes)
        qs.append(jnp.concatenate([jnp.where(head_of_lane == h, q2, jnp.zeros_like(q2)),
                                   bias_lanes.astype(BF16)], axis=1))

    def units_of(key_blocks):
        offs = [pl.multiple_of(kb * blk, blk) for kb in key_blocks]
        return [(o, h) for o in offs for h in range(heads_per)]

    def logit_thunks(key_blocks):
        return [functools.partial(lambda o, h: _dot(qs[h], kt_ref[:, pl.ds(o, blk)]), o, h)
                for o, h in units_of(key_blocks)]

    def visit(zs, key_blocks, mask, first, overlapped=None):
        units = units_of(key_blocks)
        values = [functools.partial(lambda o, w: _dot_nt(w, vt_ref[:, pl.ds(o, blk)]), o)
                  for o, _ in units]
        results = _sb_attend(zs, values, u_neg, mask, overlapped)
        for (pv, total), (o, h) in zip(results, units):
            if first:
                acc_ref[h] = pv
                s_ref[h] = total
            else:
                acc_ref[h] += jnp.exp2(s_ref[h]) * pv
                s_ref[h] += total

    r = lax.broadcasted_iota(jnp.int32, (blk, blk), 0)
    c = lax.broadcasted_iota(jnp.int32, (blk, blk), 1)
    visit(logit_thunks([i]), [i], c < r, True)

    done = 0
    run = 1
    while run < SB_KEY_BLOCKS_PER_STEP:
        top = i - done

        @pl.when(i & run != 0)
        def _(top=top, run=run):
            blocks = [top - 1 - j for j in range(run)]
            visit(logit_thunks(blocks), blocks, None, False)

        done = done + (i & run)
        run *= 2

    n_steps = i // SB_KEY_BLOCKS_PER_STEP
    n_units = SB_KEY_BLOCKS_PER_STEP * heads_per

    def step_blocks(step):
        top = i - i % SB_KEY_BLOCKS_PER_STEP - SB_KEY_BLOCKS_PER_STEP * step
        return [jnp.maximum(top - 1 - j, 0) for j in range(SB_KEY_BLOCKS_PER_STEP)]

    def store_logits(u, thunk):
        z_ref[u] = thunk()

    @pl.when(n_steps > 0)
    def _():
        for u, thunk in enumerate(logit_thunks(step_blocks(0))):
            store_logits(u, thunk)

    def body(step, carry):
        refill = [functools.partial(store_logits, u, thunk)
                  for u, thunk in enumerate(logit_thunks(step_blocks(step + 1)))]
        current = [functools.partial(lambda u: z_ref[u], u) for u in range(n_units)]
        visit(current, step_blocks(step), None, False, refill)
        return carry

    lax.fori_loop(0, n_steps, body, 0)
    out = acc_ref[0]
    for h in range(1, heads_per):
        out = jnp.where(head_of_lane == h, acc_ref[h], out)
    o_ref[...] = out


def _sb_prompt(q_bf, kt_bf, vt_bf, bias2, head_dim):
    n, width = q_bf.shape
    n_groups = width // LANES
    blk = _tile(n, 256)
    parts, rest = [], bias2
    for _ in range(SB_BIAS_PARTS):
        part = rest.astype(BF16).astype(F32)
        parts.append(part)
        rest = rest - part
    bias_parts = jnp.stack(parts, axis=1)
    return pl.pallas_call(
        functools.partial(_sb_prompt_kernel, blk=blk, head_dim=head_dim),
        grid=(n_groups, n // blk),
        in_specs=[pl.BlockSpec(memory_space=pltpu.SMEM),
                  pl.BlockSpec((blk, LANES), lambda p, i: (i, p)),
                  pl.BlockSpec((LANES + SB_BIAS_ROWS, n), lambda p, i: (p, 0)),
                  pl.BlockSpec((LANES, n), lambda p, i: (p, 0))],
        out_specs=pl.BlockSpec((blk, LANES), lambda p, i: (i, p)),
        out_shape=jax.ShapeDtypeStruct((n, width), F32),
        scratch_shapes=[pltpu.VMEM((LANES // head_dim, blk, LANES), F32),
                        pltpu.VMEM((LANES // head_dim, blk, 1), F32),
                        pltpu.VMEM((SB_KEY_BLOCKS_PER_STEP * (LANES // head_dim), blk, blk), F32)],
        compiler_params=_params("parallel", "arbitrary"),
        name="sb_prompt",
    )(bias_parts, q_bf, kt_bf, vt_bf)


def _sb_sample_kernel(pt_ref, qbd_ref, knew_ref, vnew_ref, bias_ref, kpool_ref, vpool_ref,
                      o_ref, kbuf, vbuf, sem, pad_k, pad_v, acc_ref,
                      *, n_batch, n_pages, page, group, sub, t_new):
    b = pl.program_id(0)
    n_steps = n_pages // group
    total_steps = n_batch * n_steps
    ahead = SB_PAGE_SLOTS - 1

    def copies(gstep):
        seq = gstep // n_steps
        first = (n_steps - 1 - gstep % n_steps) * group
        slot = gstep % SB_PAGE_SLOTS
        out = []
        for r in range(group):
            phys = pt_ref[seq, first + r]
            out.append(pltpu.make_async_copy(kpool_ref.at[phys], kbuf.at[slot, r], sem.at[0, slot]))
            out.append(pltpu.make_async_copy(vpool_ref.at[phys], vbuf.at[slot, r], sem.at[1, slot]))
        return out

    @pl.when(b == 0)
    def _():
        for gstep in range(min(ahead, total_steps)):
            for cp in copies(gstep):
                cp.start()

    qbd = qbd_ref[0]
    bias = bias_ref[...]
    rows = qbd.shape[0]

    pad_k[...] = jnp.zeros(pad_k.shape, F32)
    pad_v[...] = jnp.zeros(pad_v.shape, F32)
    pad_k[0:t_new, :] = knew_ref[0]
    pad_v[0:t_new, :] = vnew_ref[0]
    t_of_row = lax.broadcasted_iota(jnp.int32, (rows, page), 0) % SB_ROWS_PER_HEAD
    s_of_col = lax.broadcasted_iota(jnp.int32, (rows, page), 1)
    (pv, s_after), = _sb_attend([_dot_nt(qbd, pad_k[...].astype(BF16)) + bias],
                                [lambda w: _dot(w, pad_v[...].astype(BF16))], _u_neg(page),
                                s_of_col < t_of_row)
    acc_ref[...] = pv

    per_sub = sub // page
    u_neg = _u_neg(sub)

    def body(step, s_after):
        gstep = b * n_steps + step
        slot = gstep % SB_PAGE_SLOTS
        for cp in copies(gstep):
            cp.wait()

        @pl.when(gstep + ahead < total_steps)
        def _():
            for cp in copies(gstep + ahead):
                cp.start()

        def keys_of(buf, first):
            return jnp.concatenate([buf[slot, first + j].astype(BF16) for j in range(per_sub)],
                                   axis=1)

        firsts = list(reversed(range(0, group, per_sub)))
        zs = [_dot(qbd, keys_of(kbuf, f)) + bias for f in firsts]
        values = [functools.partial(lambda f, w: _dot_nt(w, keys_of(vbuf, f)), f) for f in firsts]
        upd = jnp.zeros(acc_ref.shape, F32)
        for pv, total in _sb_attend(zs, values, u_neg, None):
            upd = upd + jnp.exp2(s_after) * pv
            s_after = s_after + total
        acc_ref[...] += upd
        return s_after

    lax.fori_loop(0, n_steps, body, s_after)

    width = acc_ref.shape[1]
    n_heads = rows // SB_ROWS_PER_HEAD
    head_dim = width // n_heads
    head_of_row = lax.broadcasted_iota(jnp.int32, (rows, width), 0) // SB_ROWS_PER_HEAD
    head_of_col = lax.broadcasted_iota(jnp.int32, (rows, width), 1) // head_dim
    own = jnp.where(head_of_row == head_of_col, acc_ref[...], 0.0)
    out = own[0:SB_ROWS_PER_HEAD, :]
    for h in range(1, n_heads):
        out = out + own[h * SB_ROWS_PER_HEAD:(h + 1) * SB_ROWS_PER_HEAD, :]
    o_ref[0] = out[0:t_new, :]


def _sb_sample(q, k_new, v_new, bias2, pool_kt, pool_vt, page_table, q_scale):
    b, t_new, width = q.shape
    n_heads = bias2.shape[0]
    head_dim = width // n_heads
    page = pool_kt.shape[2]
    n_pages = page_table.shape[1]
    group = max(g for g in (4, 2, 1) if n_pages % g == 0)
    sub = min(2, group) * page
    rows = n_heads * SB_ROWS_PER_HEAD
    head_of_col = jnp.arange(width) // head_dim
    qpad = jnp.pad(q * q_scale, ((0, 0), (0, SB_ROWS_PER_HEAD - t_new), (0, 0)))
    qbd = jnp.where(head_of_col[None, None, None, :] == jnp.arange(n_heads)[None, :, None, None],
                    qpad[:, None, :, :], 0.0).reshape(b, rows, width).astype(BF16)
    bias_rows = jnp.repeat(bias2, SB_ROWS_PER_HEAD).reshape(rows, 1)
    return pl.pallas_call(
        functools.partial(_sb_sample_kernel, n_batch=b, n_pages=n_pages, page=page, group=group,
                          sub=sub, t_new=t_new),
        grid_spec=pltpu.PrefetchScalarGridSpec(
            num_scalar_prefetch=1,
            grid=(b,),
            in_specs=[pl.BlockSpec((1, rows, width), lambda i, pt: (i, 0, 0)),
                      pl.BlockSpec((1, t_new, width), lambda i, pt: (i, 0, 0)),
                      pl.BlockSpec((1, t_new, width), lambda i, pt: (i, 0, 0)),
                      pl.BlockSpec((rows, 1), lambda i, pt: (0, 0)),
                      pl.BlockSpec(memory_space=pl.ANY),
                      pl.BlockSpec(memory_space=pl.ANY)],
            out_specs=pl.BlockSpec((1, t_new, width), lambda i, pt: (i, 0, 0)),
            scratch_shapes=[pltpu.VMEM((SB_PAGE_SLOTS, group, width, page), F32),
                            pltpu.VMEM((SB_PAGE_SLOTS, group, width, page), F32),
                            pltpu.SemaphoreType.DMA((2, SB_PAGE_SLOTS)),
                            pltpu.VMEM((page, width), F32),
                            pltpu.VMEM((page, width), F32),
                            pltpu.VMEM((rows, width), F32)]),
        out_shape=jax.ShapeDtypeStruct((b, t_new, width), F32),
        compiler_params=_params("arbitrary"),
        name="sb_sample",
    )(page_table, qbd, k_new, v_new, bias_rows, pool_kt, pool_vt)


def _feature_major(t):
    lead = t.shape[:-3]
    tokens, heads, dim = t.shape[-3:]
    n = len(lead)
    return jnp.transpose(t, (*range(n), n + 1, n + 2, n)).reshape(*lead, heads * dim, tokens)


def _token_major(t, heads):
    lead = t.shape[:-2]
    width, tokens = t.shape[-2:]
    n = len(lead)
    t = t.reshape(*lead, heads, width // heads, tokens)
    return jnp.transpose(t, (*range(n), n + 2, n, n + 1))


def kernel(x_prompt, x_sample, state_conv, cache_sb_k, cache_sb_v, cache_mem_k, cache_mem_v, page_table, mem_prompt, a_w_in, a_b_in, a_conv_w, a_conv_b, a_norm_g, a_norm_b, a_w_out, a_b_out, b_w_in, b_b_in, b_sb_bias, b_w_out, b_b_out, w_mem_kv, ln_mix_g, ln_mix_b, ln_ffn_g, ln_ffn_b, w_router, b_router, w_exp_gate, w_exp_up, w_exp_down):
    depth = w_mem_kv.shape[0]
    n_mixers = 2
    alpha = (2.0 * depth) ** 0.25
    bp, seq, d = x_prompt.shape
    bs, t_new, _ = x_sample.shape
    mem_heads, mem_hd = cache_mem_k.shape[-2:]
    mem_w = mem_heads * mem_hd
    sb_heads, sb_hd = cache_sb_k.shape[-2:]
    sb_w = sb_heads * sb_hd
    conv_ch = a_conv_w.shape[-1]
    mem_tokens = mem_prompt.shape[1]
    assert bp == 1
    sb_q_scale = float(sb_hd) ** -0.5 * LOG2E

    y_p = x_prompt.reshape(seq, d)
    y_s = x_sample.reshape(bs * t_new, d)
    mem_p = mem_prompt.reshape(mem_tokens, d)
    conv_p, conv_s, sbk_p, sbv_p, sbk_s, sbv_s, memk_p, memv_p = ([] for _ in range(8))

    for layer in range(depth):
        kvt = _linear_t(mem_p, w_mem_kv[layer].T.astype(BF16), name="mem_kv")
        mkt_p = kvt[:mem_w][None]
        mvt_p = kvt[mem_w:][None]
        memk_p.append(_token_major(mkt_p, mem_heads))
        memv_p.append(_token_major(mvt_p, mem_heads))
        mkt_s = _feature_major(cache_mem_k[layer])
        mvt_s = _feature_major(cache_mem_v[layer])
        i = layer // n_mixers
        if layer % n_mixers == 0:
            w_in = a_w_in[i].astype(BF16)
            proj_p = _linear(y_p, w_in, a_b_in[i], name="conv_in_proj")
            proj_s = _linear(y_s, w_in, a_b_in[i], name="conv_in_proj_s")
            mix_p, st_p = _conv_prompt(proj_p, a_conv_w[i], a_conv_b[i], a_norm_g[i], a_norm_b[i])
            conv_p.append(st_p[None])
            proj_tm = jnp.transpose(proj_s.reshape(bs, t_new, -1), (1, 0, 2))
            mix_s, st_s = _conv_sample(proj_tm, jnp.transpose(state_conv[i], (1, 0, 2)),
                                       a_conv_w[i], a_conv_b[i], a_norm_g[i], a_norm_b[i])
            mix_s = jnp.transpose(mix_s, (1, 0, 2)).reshape(bs * t_new, conv_ch)
            conv_s.append(jnp.transpose(st_s, (1, 0, 2)))
            q_col = (2 * conv_ch) // mem_w
            assert q_col * mem_w == 2 * conv_ch
            qm_p, qm_s = proj_p[None], proj_s.reshape(bs, t_new, -1)
            w_out, b_out = a_w_out[i], a_b_out[i]
        else:
            bias2 = b_sb_bias[i] * LOG2E
            q_bf, kt, vt, kt_bf, vt_bf, qm = _sb_in_proj(y_p, b_w_in[i], b_b_in[i], sb_w, sb_q_scale)
            sbk_p.append(_token_major(kt[None], sb_heads))
            sbv_p.append(_token_major(vt[None], sb_heads))
            mix_p = _sb_prompt(q_bf, kt_bf, vt_bf, bias2, sb_hd)
            proj_s = _linear(y_s, b_w_in[i].astype(BF16), b_b_in[i], name="sb_in_proj_s")
            proj_s3 = proj_s.reshape(bs, t_new, -1)
            k_s = proj_s3[:, :, sb_w:2 * sb_w]
            v_s = proj_s3[:, :, 2 * sb_w:3 * sb_w]
            sbk_s.append(k_s.reshape(bs, t_new, sb_heads, sb_hd))
            sbv_s.append(v_s.reshape(bs, t_new, sb_heads, sb_hd))
            mix_s = _sb_sample(proj_s3[:, :, :sb_w], k_s, v_s, bias2,
                               _feature_major(cache_sb_k[i]), _feature_major(cache_sb_v[i]),
                               page_table, sb_q_scale)
            mix_s = mix_s.reshape(bs * t_new, sb_w)
            q_col = (3 * sb_w) // mem_w
            assert q_col * mem_w == 3 * sb_w
            qm_p, qm_s = qm[None], proj_s3
            q_col_p = 0
            w_out, b_out = b_w_out[i], b_b_out[i]

        mem_out_p = _mem_attn(qm_p, q_col if layer % n_mixers == 0 else q_col_p, mkt_p, mvt_p,
                              mem_heads)
        mem_out_s = _mem_attn(qm_s, q_col, mkt_s, mvt_s, mem_heads)
        w_out_bf = w_out.astype(BF16)
        y_p = _out_proj(y_p, mix_p, mem_out_p.reshape(seq, mem_w), w_out_bf, b_out,
                        ln_mix_g[layer], ln_mix_b[layer], alpha)
        y_s = _out_proj(y_s, mix_s, mem_out_s.reshape(bs * t_new, mem_w), w_out_bf, b_out,
                        ln_mix_g[layer], ln_mix_b[layer], alpha)

        wg, wu, wd = (w[layer].astype(BF16) for w in (w_exp_gate, w_exp_up, w_exp_down))
        y_p = _moe_grouped(y_p, _router(y_p, w_router, b_router), wg, wu, wd,
                           ln_ffn_g[layer], ln_ffn_b[layer], alpha)
        y_s = _moe(y_s, _router(y_s, w_router, b_router).T, wg, wu, wd,
                   ln_ffn_g[layer], ln_ffn_b[layer], alpha)

    return (y_p.reshape(bp, seq, d), y_s.reshape(bs, t_new, d),
            jnp.stack(conv_p), jnp.stack(conv_s), jnp.stack(sbk_p), jnp.stack(sbv_p),
            jnp.stack(sbk_s), jnp.stack(sbv_s), jnp.stack(memk_p), jnp.stack(memv_p))
```
